```python
import math
import jax
import jax.numpy as jnp
from jax import lax
import numpy as np

D_MODEL = 2048
BATCH = 1
SEQ = 16384
DEPTH = 4

CTX_LEN = 256
GRID_W = 64

S5_WIDTH = 512
S5_GROUP_CH = 16
S5_GROUPS = S5_WIDTH // S5_GROUP_CH
S5_STATE = 64
S5_DT_MIN = 0.001
S5_DT_MAX = 0.1

SSD_HEADDIM = 64
SSD_HEADS = 24
SSD_WIDTH = SSD_HEADS * SSD_HEADDIM
SSD_GROUPS = 4
SSD_HPG = SSD_HEADS // SSD_GROUPS
SSD_STATE = 128
SSD_BC = SSD_GROUPS * SSD_STATE
SSD_CONV = 5
SSD_CHUNK = 128
SSD_DT_MIN = 0.001
SSD_DT_MAX = 0.1

OFF_U = 0
OFF_X = OFF_U + S5_WIDTH
OFF_B = OFF_X + SSD_WIDTH
OFF_DT = OFF_B + SSD_BC
OFF_C = OFF_DT + 2 * SSD_HEADS
OFF_Z = OFF_C + SSD_BC
OFF_GATE_S5 = OFF_Z + SSD_WIDTH
OFF_GATE_SSD = OFF_GATE_S5 + D_MODEL
PROJ_WIDTH = OFF_GATE_SSD + D_MODEL
STATE_COLS = OFF_C
CONV_CH = SSD_WIDTH + 2 * SSD_BC

N_EXPERTS = 16
CAPACITY_FACTOR = 2
EXPERT_FF = 1024

N_MOD = 6
LN_EPS = 1e-5
ALPHA = (2 * DEPTH) ** 0.25
BETA = (8 * DEPTH) ** -0.25

kernel_name = 'hybrid_s5_ssd_expert_choice_dit'


def layer_norm(x):
    xf = x.astype(jnp.float32)
    mu = jnp.mean(xf, axis=-1, keepdims=True)
    var = jnp.mean(jnp.square(xf - mu), axis=-1, keepdims=True)
    return ((xf - mu) * lax.rsqrt(var + LN_EPS)).astype(x.dtype)


def layer_norm_affine(x, g, b):
    return layer_norm(x) * g + b


def rms_norm(x, g):
    xf = x.astype(jnp.float32)
    y = xf * lax.rsqrt(jnp.mean(jnp.square(xf), axis=-1, keepdims=True) + LN_EPS)
    return y.astype(x.dtype) * g


def modulate(x, shift, scale):
    return layer_norm(x) * (1.0 + scale) + shift


def flip(t, rev):
    return t[:, ::-1] if (rev and t is not None) else t


def to_colmajor(t, rows):
    b, l, ch = t.shape
    return t.reshape(b, rows, GRID_W, ch).transpose(0, 2, 1, 3).reshape(b, l, ch)


def from_colmajor(t, rows):
    b, l, ch = t.shape
    return t.reshape(b, GRID_W, rows, ch).transpose(0, 2, 1, 3).reshape(b, l, ch)


def depthwise_conv_centred(x, w, b):
    pad = w.shape[0] // 2
    y = lax.conv_general_dilated(x, w[:, None, :].astype(x.dtype), window_strides=(1,),
                                 padding=[(pad, pad)], dimension_numbers=('NWC', 'WIO', 'NWC'),
                                 feature_group_count=x.shape[-1])
    return y + b


def s5_discretise(lam_re, lam_im, log_step, b_re, b_im):
    lam_re, lam_im, log_step, b_re, b_im = [t.astype(jnp.float32) for t in (lam_re, lam_im, log_step, b_re, b_im)]
    step = jnp.exp(log_step)[:, None]
    mag = jnp.exp(lam_re * step)
    ab_re = mag * jnp.cos(lam_im * step)
    ab_im = mag * jnp.sin(lam_im * step)
    den = jnp.square(lam_re) + jnp.square(lam_im)
    q_re = ((ab_re - 1.0) * lam_re + ab_im * lam_im) / den
    q_im = (ab_im * lam_re - (ab_re - 1.0) * lam_im) / den
    bb_re = q_re[..., None] * b_re - q_im[..., None] * b_im
    bb_im = q_re[..., None] * b_im + q_im[..., None] * b_re
    return ab_re, ab_im, bb_re, bb_im


def complex_linear_combine(e1, e2):
    a1r, a1i, b1r, b1i = e1
    a2r, a2i, b2r, b2i = e2
    return (a1r * a2r - a1i * a2i, a1r * a2i + a1i * a2r,
            a2r * b1r - a2i * b1i + b2r, a2r * b1i + a2i * b1r + b2i)


def s5_scan(u, ab_re, ab_im, bb_re, bb_im, s0_re, s0_im):
    bu_re = jnp.einsum('blgh,gph->blgp', u, bb_re)
    bu_im = jnp.einsum('blgh,gph->blgp', u, bb_im)
    bu_re = bu_re.at[:, 0].add(ab_re * s0_re - ab_im * s0_im)
    bu_im = bu_im.at[:, 0].add(ab_re * s0_im + ab_im * s0_re)
    a_re = jnp.broadcast_to(ab_re, bu_re.shape)
    a_im = jnp.broadcast_to(ab_im, bu_im.shape)
    _, _, x_re, x_im = lax.associative_scan(complex_linear_combine, (a_re, a_im, bu_re, bu_im), axis=1)
    return x_re, x_im


def s5_readout(x_re, x_im, c_re, c_im):
    return jnp.einsum('blgp,ghp->blgh', x_re, c_re) - jnp.einsum('blgp,ghp->blgh', x_im, c_im)


def s5_glu(y, w_glu):
    y = jax.nn.gelu(y)
    return y * jax.nn.sigmoid(y @ w_glu)


def s5_branch(u_lat, u_ctx, lam_re, lam_im, log_step, b_re, b_im, c_re, c_im, d, w_glu, ctx_out):
    bt, seq, _ = u_lat.shape
    n_ctx = u_ctx.shape[1]
    ul = u_lat.astype(jnp.float32).reshape(bt, seq, S5_GROUPS, S5_GROUP_CH)
    uc = u_ctx.astype(jnp.float32).reshape(bt, n_ctx, S5_GROUPS, S5_GROUP_CH)
    dg = d.astype(jnp.float32).reshape(S5_GROUPS, S5_GROUP_CH)
    y_lat = ul * dg
    y_ctx = uc * dg if ctx_out else None
    zero = jnp.zeros((bt, S5_GROUPS, S5_STATE), jnp.float32)
    for k, rev in enumerate((False, True)):
        ab_re, ab_im, bb_re, bb_im = s5_discretise(lam_re[k], lam_im[k], log_step[k], b_re[k], b_im[k])
        cr = c_re[k].astype(jnp.float32)
        ci = c_im[k].astype(jnp.float32)
        xc_re, xc_im = s5_scan(flip(uc, rev), ab_re, ab_im, bb_re, bb_im, zero, zero)
        xl_re, xl_im = s5_scan(flip(ul, rev), ab_re, ab_im, bb_re, bb_im, xc_re[:, -1], xc_im[:, -1])
        y_lat = y_lat + flip(s5_readout(xl_re, xl_im, cr, ci), rev)
        if ctx_out:
            y_ctx = y_ctx + flip(s5_readout(xc_re, xc_im, cr, ci), rev)
    out_lat = s5_glu(y_lat.reshape(bt, seq, S5_WIDTH).astype(u_lat.dtype), w_glu)
    if not ctx_out:
        return out_lat, None
    out_ctx = s5_glu(y_ctx.reshape(bt, n_ctx, S5_WIDTH).astype(u_ctx.dtype), w_glu)
    return out_lat, out_ctx


def ssd_scan(x, dt, a, b, c, s0):
    bt, seq, g, r, p = x.shape
    n = b.shape[-1]
    nc = seq // SSD_CHUNK
    xc = x.reshape(bt, nc, SSD_CHUNK, g, r, p)
    dtc = dt.reshape(bt, nc, SSD_CHUNK, g, r)
    bc = b.reshape(bt, nc, SSD_CHUNK, g, n)
    acum = jnp.cumsum(dtc * a, axis=2)
    a_last = acum[:, :, -1]
    w_state = jnp.exp(a_last[:, :, None] - acum) * dtc
    chunk_states = jnp.einsum('bcsgn,bcsgr,bcsgrp->bcgrpn', bc, w_state, xc)

    def step(s, inp):
        st, decay = inp
        return decay[..., None, None] * s + st, s

    s_final, s_in = lax.scan(step, s0, (jnp.moveaxis(chunk_states, 1, 0), jnp.moveaxis(jnp.exp(a_last), 1, 0)))
    if c is None:
        return None, s_final
    s_in = jnp.moveaxis(s_in, 0, 1)
    cc = c.reshape(bt, nc, SSD_CHUNK, g, n)
    seg = acum[:, :, :, None] - acum[:, :, None]
    lower = jnp.tril(jnp.ones((SSD_CHUNK, SSD_CHUNK), dtype=bool))[:, :, None, None]
    decay = jnp.exp(jnp.where(lower, seg, -jnp.inf))
    cb = jnp.einsum('bcqgn,bcsgn->bcqsg', cc, bc)
    y_diag = jnp.einsum('bcqsgr,bcsgrp->bcqgrp', cb[..., None] * decay * dtc[:, :, None], xc)
    y_off = jnp.einsum('bcqgn,bcgrpn->bcqgrp', cc, s_in) * jnp.exp(acum)[..., None]
    return (y_diag + y_off).reshape(bt, seq, g, r, p), s_final


def split_xbc(xbc):
    bt, l, ch = xbc.shape
    xbc = xbc.astype(jnp.float32)
    xs = xbc[..., :SSD_WIDTH].reshape(bt, l, SSD_GROUPS, SSD_HPG, SSD_HEADDIM)
    bs = xbc[..., SSD_WIDTH:SSD_WIDTH + SSD_BC].reshape(bt, l, SSD_GROUPS, SSD_STATE)
    cs = xbc[..., SSD_WIDTH + SSD_BC:].reshape(bt, l, SSD_GROUPS, SSD_STATE) if ch == CONV_CH else None
    return xs, bs, cs


def ssd_branch(p_lat, p_ctx, rows, conv_w, conv_b, a_log, dt_bias, d, norm_w, ctx_out):
    bt, seq, _ = p_lat.shape
    n_ctx = p_ctx.shape[1]
    xb_w = OFF_DT - OFF_X
    s_lat = to_colmajor(p_lat[..., OFF_X:OFF_Z], rows)
    xbc_lat = jnp.concatenate([s_lat[..., :xb_w], s_lat[..., OFF_C - OFF_X:]], axis=-1)
    dt_lat = s_lat[..., xb_w:OFF_C - OFF_X]
    xbc_lat = jax.nn.silu(depthwise_conv_centred(xbc_lat, conv_w, conv_b))
    if ctx_out:
        xbc_ctx = jnp.concatenate([p_ctx[..., OFF_X:OFF_DT], p_ctx[..., OFF_C:OFF_Z]], axis=-1)
        xbc_ctx = jax.nn.silu(depthwise_conv_centred(xbc_ctx, conv_w, conv_b))
    else:
        xbc_ctx = jax.nn.silu(depthwise_conv_centred(p_ctx[..., OFF_X:OFF_DT], conv_w[:, :xb_w], conv_b[:xb_w]))
    dt_ctx = p_ctx[..., OFF_DT:OFF_C]
    x_l, b_l, c_l = split_xbc(xbc_lat)
    x_c, b_c, c_c = split_xbc(xbc_ctx)
    d_skip = d.astype(jnp.float32).reshape(SSD_GROUPS, SSD_HPG, 1)
    y_lat = x_l * d_skip
    y_ctx = x_c * d_skip if ctx_out else None
    zero = jnp.zeros((bt, SSD_GROUPS, SSD_HPG, SSD_HEADDIM, SSD_STATE), jnp.float32)
    for k, rev in enumerate((False, True)):
        a = -jnp.exp(a_log[k].astype(jnp.float32)).reshape(SSD_GROUPS, SSD_HPG)
        dtb = dt_bias[k].astype(jnp.float32)
        lo, hi = k * SSD_HEADS, (k + 1) * SSD_HEADS
        dt_l = jax.nn.softplus(dt_lat[..., lo:hi].astype(jnp.float32) + dtb).reshape(bt, seq, SSD_GROUPS, SSD_HPG)
        dt_c = jax.nn.softplus(dt_ctx[..., lo:hi].astype(jnp.float32) + dtb).reshape(bt, n_ctx, SSD_GROUPS, SSD_HPG)
        yc, s_ctx = ssd_scan(flip(x_c, rev), flip(dt_c, rev), a, flip(b_c, rev), flip(c_c, rev), zero)
        yl, _ = ssd_scan(flip(x_l, rev), flip(dt_l, rev), a, flip(b_l, rev), flip(c_l, rev), s_ctx)
        y_lat = y_lat + flip(yl, rev)
        if ctx_out:
            y_ctx = y_ctx + flip(yc, rev)
    y_lat = from_colmajor(y_lat.reshape(bt, seq, SSD_WIDTH), rows).astype(p_lat.dtype)
    out_lat = rms_norm(y_lat * jax.nn.silu(p_lat[..., OFF_Z:OFF_GATE_S5]), norm_w)
    if not ctx_out:
        return out_lat, None
    y_ctx = y_ctx.reshape(bt, n_ctx, SSD_WIDTH).astype(p_ctx.dtype)
    out_ctx = rms_norm(y_ctx * jax.nn.silu(p_ctx[..., OFF_Z:OFF_GATE_S5]), norm_w)
    return out_lat, out_ctx


def token_mixer(h_lat, h_ctx, w_in, s5_lambda_re, s5_lambda_im, s5_log_step, s5_b_re, s5_b_im, s5_c_re, s5_c_im,
                s5_d, s5_w_glu, ssd_conv_w, ssd_conv_b, ssd_a_log, ssd_dt_bias, ssd_d, ssd_norm_w,
                w_branch_s5, w_branch_ssd, w_out, ctx_out):
    rows = h_lat.shape[1] // GRID_W
    p_lat = h_lat @ w_in
    p_ctx = h_ctx @ (w_in if ctx_out else w_in[:, :STATE_COLS])
    a_lat, a_ctx = s5_branch(p_lat[..., OFF_U:OFF_X], p_ctx[..., OFF_U:OFF_X], s5_lambda_re, s5_lambda_im,
                             s5_log_step, s5_b_re, s5_b_im, s5_c_re, s5_c_im, s5_d, s5_w_glu, ctx_out)
    b_lat, b_ctx = ssd_branch(p_lat, p_ctx, rows, ssd_conv_w, ssd_conv_b, ssd_a_log, ssd_dt_bias, ssd_d,
                              ssd_norm_w, ctx_out)

    def merge(p, ya, yb):
        g_a = jax.nn.sigmoid(p[..., OFF_GATE_S5:OFF_GATE_SSD])
        g_b = jax.nn.sigmoid(p[..., OFF_GATE_SSD:PROJ_WIDTH])
        return (g_a * (ya @ w_branch_s5) + g_b * (yb @ w_branch_ssd)) @ w_out

    m_lat = merge(p_lat, a_lat, b_lat)
    m_ctx = merge(p_ctx, a_ctx, b_ctx) if ctx_out else None
    return m_lat, m_ctx


def expert_choice_single(h, w_router, w_gate, w_up, w_down):
    n_tok = h.shape[0]
    cap = CAPACITY_FACTOR * n_tok // N_EXPERTS
    aff = jax.nn.softmax((h @ w_router).astype(jnp.float32), axis=-1)
    g, idx = lax.top_k(aff.T, cap)
    xe = h[idx]
    hid = jax.nn.silu(jnp.einsum('ecd,edf->ecf', xe, w_gate)) * jnp.einsum('ecd,edf->ecf', xe, w_up)
    ye = jnp.einsum('ecf,efd->ecd', hid, w_down) * g[..., None].astype(h.dtype)
    return jnp.zeros_like(h).at[idx.reshape(-1)].add(ye.reshape(-1, h.shape[-1]))


def expert_choice_ffn(h, w_router, w_gate, w_up, w_down):
    return jax.vmap(expert_choice_single, in_axes=(0, None, None, None, None))(h, w_router, w_gate, w_up, w_down)


def setup_inputs(seed: int = 0) -> dict:
    key = jax.random.key(seed)
    ks = jax.random.split(key, 33)
    f32 = jnp.float32

    def nrm(k, shape, scale):
        return jax.random.normal(k, shape, f32) * scale

    def ones_noise(k, shape):
        return 1.0 + nrm(k, shape, 0.01)

    L = DEPTH
    s5_shape = (L, 2, S5_GROUPS, S5_STATE)
    dt0 = jnp.exp(jax.random.uniform(ks[19], (L, 2, SSD_HEADS), f32, math.log(SSD_DT_MIN), math.log(SSD_DT_MAX)))
    return {
        'x': nrm(ks[0], (BATCH, SEQ, D_MODEL), 1.0),
        'c': nrm(ks[1], (BATCH, D_MODEL), 1.0),
        'ctx': nrm(ks[2], (BATCH, CTX_LEN, D_MODEL), 1.0),
        'c_ctx': nrm(ks[3], (D_MODEL,), 1.0),
        'w_ada': nrm(ks[4], (L, D_MODEL, N_MOD * D_MODEL), 0.5 * D_MODEL ** -0.5),
        'b_ada': nrm(ks[5], (L, N_MOD * D_MODEL), 0.01),
        'w_in': nrm(ks[6], (L, D_MODEL, PROJ_WIDTH), D_MODEL ** -0.5),
        's5_lambda_re': -0.5 + nrm(ks[7], s5_shape, 0.01),
        's5_lambda_im': math.pi * jnp.arange(S5_STATE, dtype=f32) + nrm(ks[8], s5_shape, 0.01),
        's5_log_step': jax.random.uniform(ks[9], (L, 2, S5_GROUPS), f32, math.log(S5_DT_MIN), math.log(S5_DT_MAX)),
        's5_b_re': nrm(ks[10], (L, 2, S5_GROUPS, S5_STATE, S5_GROUP_CH), (2 * S5_GROUP_CH) ** -0.5),
        's5_b_im': nrm(ks[11], (L, 2, S5_GROUPS, S5_STATE, S5_GROUP_CH), (2 * S5_GROUP_CH) ** -0.5),
        's5_c_re': nrm(ks[12], (L, 2, S5_GROUPS, S5_GROUP_CH, S5_STATE), (2 * S5_STATE) ** -0.5),
        's5_c_im': nrm(ks[13], (L, 2, S5_GROUPS, S5_GROUP_CH, S5_STATE), (2 * S5_STATE) ** -0.5),
        's5_d': nrm(ks[14], (L, S5_WIDTH), 1.0),
        's5_w_glu': nrm(ks[15], (L, S5_WIDTH, S5_WIDTH), S5_WIDTH ** -0.5),
        'ssd_conv_w': nrm(ks[16], (L, SSD_CONV, CONV_CH), SSD_CONV ** -0.5),
        'ssd_conv_b': nrm(ks[17], (L, CONV_CH), 0.01),
        'ssd_a_log': jnp.log(jax.random.uniform(ks[18], (L, 2, SSD_HEADS), f32, 1.0, 16.0)),
        'ssd_dt_bias': dt0 + jnp.log(-jnp.expm1(-dt0)),
        'ssd_d': ones_noise(ks[20], (L, SSD_HEADS)),
        'ssd_norm_w': ones_noise(ks[21], (L, SSD_WIDTH)),
        'w_branch_s5': nrm(ks[22], (L, S5_WIDTH, D_MODEL), S5_WIDTH ** -0.5),
        'w_branch_ssd': nrm(ks[23], (L, SSD_WIDTH, D_MODEL), SSD_WIDTH ** -0.5),
        'w_out': nrm(ks[24], (L, D_MODEL, D_MODEL), BETA * D_MODEL ** -0.5),
        'ln1_g': ones_noise(ks[25], (L, D_MODEL)),
        'ln1_b': nrm(ks[26], (L, D_MODEL), 0.01),
        'w_router': nrm(ks[27], (L, D_MODEL, N_EXPERTS), D_MODEL ** -0.5),
        'w_exp_gate': nrm(ks[28], (L, N_EXPERTS, D_MODEL, EXPERT_FF), D_MODEL ** -0.5),
        'w_exp_up': nrm(ks[29], (L, N_EXPERTS, D_MODEL, EXPERT_FF), D_MODEL ** -0.5),
        'w_exp_down': nrm(ks[30], (L, N_EXPERTS, EXPERT_FF, D_MODEL), BETA * EXPERT_FF ** -0.5),
        'ln2_g': ones_noise(ks[31], (L, D_MODEL)),
        'ln2_b': nrm(ks[32], (L, D_MODEL), 0.01),
    }


def reference(x, c, ctx, c_ctx, w_ada, b_ada, w_in, s5_lambda_re, s5_lambda_im, s5_log_step, s5_b_re, s5_b_im,
              s5_c_re, s5_c_im, s5_d, s5_w_glu, ssd_conv_w, ssd_conv_b, ssd_a_log, ssd_dt_bias, ssd_d, ssd_norm_w,
              w_branch_s5, w_branch_ssd, w_out, ln1_g, ln1_b, w_router, w_exp_gate, w_exp_up, w_exp_down,
              ln2_g, ln2_b):
    ctx_h = ctx
    for i in range(DEPTH):
        ctx_out = i < DEPTH - 1
        mod_l = jax.nn.silu(c) @ w_ada[i] + b_ada[i]
        mod_c = jax.nn.silu(c_ctx) @ w_ada[i] + b_ada[i]
        sh1, sc1, g1, sh2, sc2, g2 = jnp.split(mod_l[:, None, :], N_MOD, axis=-1)
        csh1, csc1, cg1, csh2, csc2, cg2 = jnp.split(mod_c, N_MOD, axis=-1)
        m_lat, m_ctx = token_mixer(modulate(x, sh1, sc1), modulate(ctx_h, csh1, csc1), w_in[i],
                                   s5_lambda_re[i], s5_lambda_im[i], s5_log_step[i], s5_b_re[i], s5_b_im[i],
                                   s5_c_re[i], s5_c_im[i], s5_d[i], s5_w_glu[i], ssd_conv_w[i], ssd_conv_b[i],
                                   ssd_a_log[i], ssd_dt_bias[i], ssd_d[i], ssd_norm_w[i],
                                   w_branch_s5[i], w_branch_ssd[i], w_out[i], ctx_out)
        x = layer_norm_affine(ALPHA * x + g1 * m_lat, ln1_g[i], ln1_b[i])
        y_moe = expert_choice_ffn(modulate(x, sh2, sc2), w_router[i], w_exp_gate[i], w_exp_up[i], w_exp_down[i])
        x = layer_norm_affine(ALPHA * x + g2 * y_moe, ln2_g[i], ln2_b[i])
        if ctx_out:
            ctx_h = layer_norm_affine(ALPHA * ctx_h + cg1 * m_ctx, ln1_g[i], ln1_b[i])
            c_moe = expert_choice_ffn(modulate(ctx_h, csh2, csc2), w_router[i], w_exp_gate[i], w_exp_up[i],
                                      w_exp_down[i])
            ctx_h = layer_norm_affine(ALPHA * ctx_h + cg2 * c_moe, ln2_g[i], ln2_b[i])
    return x
```

```python
import functools
import math

import jax
import jax.numpy as jnp
from jax import lax
from jax.experimental import pallas as pl
from jax.experimental.pallas import tpu as pltpu

f32 = jnp.float32
bf16 = jnp.bfloat16
i32 = jnp.int32

D_MODEL = 2048
DEPTH = 4
GRID_W = 64
S5_WIDTH = 512
S5_GROUP_CH = 16
S5_GROUPS = 32
S5_STATE = 64
SSD_HEADDIM = 64
SSD_HEADS = 24
SSD_WIDTH = 1536
SSD_GROUPS = 4
SSD_HPG = 6
SSD_STATE = 128
SSD_BC = 512
SSD_CHUNK = 128
OFF_U = 0
OFF_X = 512
OFF_B = 2048
OFF_DT = 2560
OFF_C = 2608
OFF_Z = 3120
OFF_GATE_S5 = 4656
OFF_GATE_SSD = 6704
PROJ_WIDTH = 8752
CONV_CH = SSD_WIDTH + 2 * SSD_BC
N_EXPERTS = 16
CAPACITY_FACTOR = 2
N_MOD = 6
LN_EPS = 1e-5
ALPHA = (2 * DEPTH) ** 0.25

PW = 9216
VMEM_LIMIT = 56 * 1024 * 1024


def _cparams(sem):
    return pltpu.CompilerParams(dimension_semantics=sem, vmem_limit_bytes=VMEM_LIMIT)


def _dot(a, b):
    return jnp.dot(a, b, preferred_element_type=f32)


def _pick(n, cands):
    for c in cands:
        if n % c == 0:
            return c
    raise ValueError((n, cands))


def _ada_kernel(c_ref, w_ref, b_ref, o_ref):
    c = c_ref[...]
    s = c / (1.0 + jnp.exp(-c))
    hi = s.astype(bf16)
    lo = (s - hi.astype(f32)).astype(bf16)
    w = w_ref[...].astype(bf16)
    o_ref[...] = _dot(hi, w) + _dot(lo, w) + b_ref[...]


def ada_all(c8, w_ada, b_ada):
    n = N_MOD * D_MODEL
    tn = 1536
    return pl.pallas_call(
        _ada_kernel,
        grid=(DEPTH, n // tn),
        in_specs=[pl.BlockSpec((8, D_MODEL), lambda l, j: (0, 0)),
                  pl.BlockSpec((None, D_MODEL, tn), lambda l, j: (l, 0, j)),
                  pl.BlockSpec((None, 1, tn), lambda l, j: (l, 0, j))],
        out_specs=pl.BlockSpec((None, 8, tn), lambda l, j: (l, 0, j)),
        out_shape=jax.ShapeDtypeStruct((DEPTH, 8, n), f32),
        compiler_params=_cparams(("arbitrary", "arbitrary")),
        name="ada",
    )(c8, w_ada, b_ada.reshape(DEPTH, 1, n))


def _proj_kernel(x_ref, mod_ref, w_ref, o_ref, h_ref, *, tm, n_lat):
    i = pl.program_id(0)
    j = pl.program_id(1)

    @pl.when(j == 0)
    def _():
        x = x_ref[...]
        mu = jnp.mean(x, axis=-1, keepdims=True)
        xc = x - mu
        var = jnp.mean(xc * xc, axis=-1, keepdims=True)
        xn = xc * lax.rsqrt(var + LN_EPS)
        row = i * tm + lax.broadcasted_iota(i32, (tm, 1), 0)
        isctx = row >= n_lat
        shift = jnp.where(isctx, mod_ref[2:3, :], mod_ref[0:1, :])
        scale = jnp.where(isctx, mod_ref[3:4, :], mod_ref[1:2, :])
        h_ref[...] = (xn * (1.0 + scale) + shift).astype(bf16)

    o_ref[...] = _dot(h_ref[...], w_ref[...])


def proj(x_all, modrows, w_pad, n_lat):
    t = x_all.shape[0]
    tm = _pick(t, (1280, 768, 512, 256))
    tn = 1024
    return pl.pallas_call(
        functools.partial(_proj_kernel, tm=tm, n_lat=n_lat),
        grid=(t // tm, PW // tn),
        in_specs=[pl.BlockSpec((tm, D_MODEL), lambda i, j: (i, 0)),
                  pl.BlockSpec((8, D_MODEL), lambda i, j: (0, 0)),
                  pl.BlockSpec((D_MODEL, tn), lambda i, j: (0, j))],
        out_specs=pl.BlockSpec((tm, tn), lambda i, j: (i, j)),
        out_shape=jax.ShapeDtypeStruct((t, PW), f32),
        scratch_shapes=[pltpu.VMEM((tm, D_MODEL), bf16)],
        compiler_params=_cparams(("arbitrary", "arbitrary")),
        name="proj",
    )(x_all, modrows, w_pad)


def _mm_kernel(a_ref, b_ref, o_ref):
    o_ref[...] = _dot(a_ref[...].astype(bf16), b_ref[...].astype(bf16))


def matmul(a, b):
    m, k = a.shape
    n = b.shape[1]
    tm = _pick(m, (1024, 512, 256, 128, 32, 8))
    tn = _pick(n, (1024, 512, 256, 128))
    return pl.pallas_call(
        _mm_kernel,
        grid=(m // tm, n // tn),
        in_specs=[pl.BlockSpec((tm, k), lambda i, j: (i, 0)),
                  pl.BlockSpec((k, tn), lambda i, j: (0, j))],
        out_specs=pl.BlockSpec((tm, tn), lambda i, j: (i, j)),
        out_shape=jax.ShapeDtypeStruct((m, n), f32),
        compiler_params=_cparams(("arbitrary", "arbitrary")),
        name="mm",
    )(a, b)


def bmm(a, b):
    e, m, k = a.shape
    n = b.shape[2]
    tm = _pick(m, (1024, 512, 256, 128, 32, 8))
    tn = _pick(n, (1024, 512, 256, 128))
    return pl.pallas_call(
        _mm_kernel,
        grid=(e, m // tm, n // tn),
        in_specs=[pl.BlockSpec((None, tm, k), lambda x, i, j: (x, i, 0)),
                  pl.BlockSpec((None, k, tn), lambda x, i, j: (x, 0, j))],
        out_specs=pl.BlockSpec((None, tm, tn), lambda x, i, j: (x, i, j)),
        out_shape=jax.ShapeDtypeStruct((e, m, n), f32),
        compiler_params=_cparams(("arbitrary", "arbitrary", "arbitrary")),
        name="bmm",
    )(a, b)


def _layer_norm(x):
    mu = jnp.mean(x, axis=-1, keepdims=True)
    var = jnp.mean(jnp.square(x - mu), axis=-1, keepdims=True)
    return (x - mu) * lax.rsqrt(var + LN_EPS)


def _layer_norm_affine(x, g, b):
    return _layer_norm(x) * g + b


def _rms_norm(x, g):
    return x * lax.rsqrt(jnp.mean(jnp.square(x), axis=-1, keepdims=True) + LN_EPS) * g


def _modulate(x, shift, scale):
    return _layer_norm(x) * (1.0 + scale) + shift


def _flip(t, rev):
    return t[:, ::-1] if (rev and t is not None) else t


def _to_colmajor(t, rows):
    b, l, ch = t.shape
    return t.reshape(b, rows, GRID_W, ch).transpose(0, 2, 1, 3).reshape(b, l, ch)


def _from_colmajor(t, rows):
    b, l, ch = t.shape
    return t.reshape(b, GRID_W, rows, ch).transpose(0, 2, 1, 3).reshape(b, l, ch)


def _dwconv(x, w, b):
    pad = w.shape[0] // 2
    y = lax.conv_general_dilated(x, w[:, None, :].astype(x.dtype), window_strides=(1,),
                                 padding=[(pad, pad)], dimension_numbers=('NWC', 'WIO', 'NWC'),
                                 feature_group_count=x.shape[-1])
    return y + b


def _s5_discretise(lam_re, lam_im, log_step, b_re, b_im):
    step = jnp.exp(log_step)[:, None]
    mag = jnp.exp(lam_re * step)
    ab_re = mag * jnp.cos(lam_im * step)
    ab_im = mag * jnp.sin(lam_im * step)
    den = jnp.square(lam_re) + jnp.square(lam_im)
    q_re = ((ab_re - 1.0) * lam_re + ab_im * lam_im) / den
    q_im = (ab_im * lam_re - (ab_re - 1.0) * lam_im) / den
    bb_re = q_re[..., None] * b_re - q_im[..., None] * b_im
    bb_im = q_re[..., None] * b_im + q_im[..., None] * b_re
    return ab_re, ab_im, bb_re, bb_im


def _clc(e1, e2):
    a1r, a1i, b1r, b1i = e1
    a2r, a2i, b2r, b2i = e2
    return (a1r * a2r - a1i * a2i, a1r * a2i + a1i * a2r,
            a2r * b1r - a2i * b1i + b2r, a2r * b1i + a2i * b1r + b2i)


def _s5_scan(u, ab_re, ab_im, bb_re, bb_im, s0_re, s0_im):
    bu_re = jnp.einsum('blgh,gph->blgp', u, bb_re)
    bu_im = jnp.einsum('blgh,gph->blgp', u, bb_im)
    bu_re = bu_re.at[:, 0].add(ab_re * s0_re - ab_im * s0_im)
    bu_im = bu_im.at[:, 0].add(ab_re * s0_im + ab_im * s0_re)
    a_re = jnp.broadcast_to(ab_re, bu_re.shape)
    a_im = jnp.broadcast_to(ab_im, bu_im.shape)
    _, _, x_re, x_im = lax.associative_scan(_clc, (a_re, a_im, bu_re, bu_im), axis=1)
    return x_re, x_im


def _s5_readout(x_re, x_im, c_re, c_im):
    return jnp.einsum('blgp,ghp->blgh', x_re, c_re) - jnp.einsum('blgp,ghp->blgh', x_im, c_im)


def _s5_glu(y, w_glu):
    y = jax.nn.gelu(y)
    bt, l, w = y.shape
    return y * jax.nn.sigmoid(matmul(y.reshape(bt * l, w), w_glu).reshape(bt, l, w))


def _s5_branch(u_lat, u_ctx, lam_re, lam_im, log_step, b_re, b_im, c_re, c_im, d, w_glu, ctx_out):
    bt, seq, _ = u_lat.shape
    n_ctx = u_ctx.shape[1]
    ul = u_lat.reshape(bt, seq, S5_GROUPS, S5_GROUP_CH)
    uc = u_ctx.reshape(bt, n_ctx, S5_GROUPS, S5_GROUP_CH)
    dg = d.reshape(S5_GROUPS, S5_GROUP_CH)
    y_lat = ul * dg
    y_ctx = uc * dg if ctx_out else None
    zero = jnp.zeros((bt, S5_GROUPS, S5_STATE), f32)
    for k, rev in enumerate((False, True)):
        ab_re, ab_im, bb_re, bb_im = _s5_discretise(lam_re[k], lam_im[k], log_step[k], b_re[k], b_im[k])
        xc_re, xc_im = _s5_scan(_flip(uc, rev), ab_re, ab_im, bb_re, bb_im, zero, zero)
        xl_re, xl_im = _s5_scan(_flip(ul, rev), ab_re, ab_im, bb_re, bb_im, xc_re[:, -1], xc_im[:, -1])
        y_lat = y_lat + _flip(_s5_readout(xl_re, xl_im, c_re[k], c_im[k]), rev)
        if ctx_out:
            y_ctx = y_ctx + _flip(_s5_readout(xc_re, xc_im, c_re[k], c_im[k]), rev)
    out_lat = _s5_glu(y_lat.reshape(bt, seq, S5_WIDTH), w_glu)
    if not ctx_out:
        return out_lat, None
    return out_lat, _s5_glu(y_ctx.reshape(bt, n_ctx, S5_WIDTH), w_glu)


def _ssd_scan(x, dt, a, b, c, s0):
    bt, seq, g, r, p = x.shape
    n = b.shape[-1]
    nc = seq // SSD_CHUNK
    xc = x.reshape(bt, nc, SSD_CHUNK, g, r, p)
    dtc = dt.reshape(bt, nc, SSD_CHUNK, g, r)
    bc = b.reshape(bt, nc, SSD_CHUNK, g, n)
    acum = jnp.cumsum(dtc * a, axis=2)
    a_last = acum[:, :, -1]
    w_state = jnp.exp(a_last[:, :, None] - acum) * dtc
    chunk_states = jnp.einsum('bcsgn,bcsgr,bcsgrp->bcgrpn', bc, w_state, xc)

    def step(s, inp):
        st, decay = inp
        return decay[..., None, None] * s + st, s

    s_final, s_in = lax.scan(step, s0, (jnp.moveaxis(chunk_states, 1, 0), jnp.moveaxis(jnp.exp(a_last), 1, 0)))
    if c is None:
        return None, s_final
    s_in = jnp.moveaxis(s_in, 0, 1)
    cc = c.reshape(bt, nc, SSD_CHUNK, g, n)
    seg = acum[:, :, :, None] - acum[:, :, None]
    lower = jnp.tril(jnp.ones((SSD_CHUNK, SSD_CHUNK), dtype=bool))[:, :, None, None]
    decay = jnp.exp(jnp.where(lower, seg, -jnp.inf))
    cb = jnp.einsum('bcqgn,bcsgn->bcqsg', cc, bc)
    y_diag = jnp.einsum('bcqsgr,bcsgrp->bcqgrp', cb[..., None] * decay * dtc[:, :, None], xc)
    y_off = jnp.einsum('bcqgn,bcgrpn->bcqgrp', cc, s_in) * jnp.exp(acum)[..., None]
    return (y_diag + y_off).reshape(bt, seq, g, r, p), s_final


def _split_xbc(xbc):
    bt, l, ch = xbc.shape
    xs = xbc[..., :SSD_WIDTH].reshape(bt, l, SSD_GROUPS, SSD_HPG, SSD_HEADDIM)
    bs = xbc[..., SSD_WIDTH:SSD_WIDTH + SSD_BC].reshape(bt, l, SSD_GROUPS, SSD_STATE)
    cs = xbc[..., SSD_WIDTH + SSD_BC:].reshape(bt, l, SSD_GROUPS, SSD_STATE) if ch == CONV_CH else None
    return xs, bs, cs


def _ssd_branch(p_lat, p_ctx, rows, conv_w, conv_b, a_log, dt_bias, d, norm_w, ctx_out):
    bt, seq, _ = p_lat.shape
    n_ctx = p_ctx.shape[1]
    xb_w = OFF_DT - OFF_X
    s_lat = _to_colmajor(p_lat[..., OFF_X:OFF_Z], rows)
    xbc_lat = jnp.concatenate([s_lat[..., :xb_w], s_lat[..., OFF_C - OFF_X:]], axis=-1)
    dt_lat = s_lat[..., xb_w:OFF_C - OFF_X]
    xbc_lat = jax.nn.silu(_dwconv(xbc_lat, conv_w, conv_b))
    if ctx_out:
        xbc_ctx = jnp.concatenate([p_ctx[..., OFF_X:OFF_DT], p_ctx[..., OFF_C:OFF_Z]], axis=-1)
        xbc_ctx = jax.nn.silu(_dwconv(xbc_ctx, conv_w, conv_b))
    else:
        xbc_ctx = jax.nn.silu(_dwconv(p_ctx[..., OFF_X:OFF_DT], conv_w[:, :xb_w], conv_b[:xb_w]))
    dt_ctx = p_ctx[..., OFF_DT:OFF_C]
    x_l, b_l, c_l = _split_xbc(xbc_lat)
    x_c, b_c, c_c = _split_xbc(xbc_ctx)
    d_skip = d.reshape(SSD_GROUPS, SSD_HPG, 1)
    y_lat = x_l * d_skip
    y_ctx = x_c * d_skip if ctx_out else None
    zero = jnp.zeros((bt, SSD_GROUPS, SSD_HPG, SSD_HEADDIM, SSD_STATE), f32)
    for k, rev in enumerate((False, True)):
        a = -jnp.exp(a_log[k]).reshape(SSD_GROUPS, SSD_HPG)
        dtb = dt_bias[k]
        lo, hi = k * SSD_HEADS, (k + 1) * SSD_HEADS
        dt_l = jax.nn.softplus(dt_lat[..., lo:hi] + dtb).reshape(bt, seq, SSD_GROUPS, SSD_HPG)
        dt_c = jax.nn.softplus(dt_ctx[..., lo:hi] + dtb).reshape(bt, n_ctx, SSD_GROUPS, SSD_HPG)
        yc, s_ctx = _ssd_scan(_flip(x_c, rev), _flip(dt_c, rev), a, _flip(b_c, rev), _flip(c_c, rev), zero)
        yl, _ = _ssd_scan(_flip(x_l, rev), _flip(dt_l, rev), a, _flip(b_l, rev), _flip(c_l, rev), s_ctx)
        y_lat = y_lat + _flip(yl, rev)
        if ctx_out:
            y_ctx = y_ctx + _flip(yc, rev)
    y_lat = _from_colmajor(y_lat.reshape(bt, seq, SSD_WIDTH), rows)
    out_lat = _rms_norm(y_lat * jax.nn.silu(p_lat[..., OFF_Z:OFF_GATE_S5]), norm_w)
    if not ctx_out:
        return out_lat, None
    y_ctx = y_ctx.reshape(bt, n_ctx, SSD_WIDTH)
    out_ctx = _rms_norm(y_ctx * jax.nn.silu(p_ctx[..., OFF_Z:OFF_GATE_S5]), norm_w)
    return out_lat, out_ctx


def _merge(p, ya, yb, w_branch_s5, w_branch_ssd, w_out):
    bt, l, _ = p.shape
    g_a = jax.nn.sigmoid(p[..., OFF_GATE_S5:OFF_GATE_SSD]).reshape(bt * l, D_MODEL)
    g_b = jax.nn.sigmoid(p[..., OFF_GATE_SSD:PROJ_WIDTH]).reshape(bt * l, D_MODEL)
    za = matmul(ya.reshape(bt * l, S5_WIDTH), w_branch_s5)
    zb = matmul(yb.reshape(bt * l, SSD_WIDTH), w_branch_ssd)
    return matmul(g_a * za + g_b * zb, w_out).reshape(bt, l, D_MODEL)


def _expert_choice(h, w_router, w_gate, w_up, w_down):
    n_tok = h.shape[0]
    cap = CAPACITY_FACTOR * n_tok // N_EXPERTS
    aff = jax.nn.softmax((h @ w_router).astype(f32), axis=-1)
    g, idx = lax.top_k(aff.T, cap)
    xe = h[idx]
    hid = jax.nn.silu(bmm(xe, w_gate)) * bmm(xe, w_up)
    ye = bmm(hid, w_down) * g[..., None]
    return jnp.zeros_like(h).at[idx.reshape(-1)].add(ye.reshape(-1, h.shape[-1]))


def kernel(x, c, ctx, c_ctx, w_ada, b_ada, w_in, s5_lambda_re, s5_lambda_im, s5_log_step, s5_b_re, s5_b_im,
           s5_c_re, s5_c_im, s5_d, s5_w_glu, ssd_conv_w, ssd_conv_b, ssd_a_log, ssd_dt_bias, ssd_d, ssd_norm_w,
           w_branch_s5, w_branch_ssd, w_out, ln1_g, ln1_b, w_router, w_exp_gate, w_exp_up, w_exp_down,
           ln2_g, ln2_b):
    bt, seq, _ = x.shape
    n_ctx = ctx.shape[1]
    rows = seq // GRID_W
    c8 = jnp.zeros((8, D_MODEL), f32).at[0].set(c[0]).at[1].set(c_ctx)
    mods = ada_all(c8, w_ada, b_ada)
    w_pad = jnp.pad(w_in, ((0, 0), (0, 0), (0, PW - PROJ_WIDTH))).astype(bf16)
    ctx_h = ctx
    for i in range(DEPTH):
        ctx_out = i < DEPTH - 1
        sh1, sc1, g1, sh2, sc2, g2 = jnp.split(mods[i, 0], N_MOD)
        csh1, csc1, cg1, csh2, csc2, cg2 = jnp.split(mods[i, 1], N_MOD)
        modrows = jnp.zeros((8, D_MODEL), f32).at[0].set(sh1).at[1].set(sc1).at[2].set(csh1).at[3].set(csc1)
        x_all = jnp.concatenate([x[0], ctx_h[0]], axis=0)
        p_all = proj(x_all, modrows, w_pad[i], seq)
        p_lat = p_all[None, :seq, :PROJ_WIDTH]
        p_ctx = p_all[None, seq:, :PROJ_WIDTH]
        a_lat, a_ctx = _s5_branch(p_lat[..., OFF_U:OFF_X], p_ctx[..., OFF_U:OFF_X], s5_lambda_re[i], s5_lambda_im[i],
                                  s5_log_step[i], s5_b_re[i], s5_b_im[i], s5_c_re[i], s5_c_im[i], s5_d[i],
                                  s5_w_glu[i], ctx_out)
        b_lat, b_ctx = _ssd_branch(p_lat, p_ctx, rows, ssd_conv_w[i], ssd_conv_b[i], ssd_a_log[i], ssd_dt_bias[i],
                                   ssd_d[i], ssd_norm_w[i], ctx_out)
        m_lat = _merge(p_lat, a_lat, b_lat, w_branch_s5[i], w_branch_ssd[i], w_out[i])
        x = _layer_norm_affine(ALPHA * x + g1 * m_lat, ln1_g[i], ln1_b[i])
        y_moe = _expert_choice(_modulate(x, sh2, sc2)[0], w_router[i], w_exp_gate[i], w_exp_up[i], w_exp_down[i])
        x = _layer_norm_affine(ALPHA * x + g2 * y_moe[None], ln2_g[i], ln2_b[i])
        if ctx_out:
            m_ctx = _merge(p_ctx, a_ctx, b_ctx, w_branch_s5[i], w_branch_ssd[i], w_out[i])
            ctx_h = _layer_norm_affine(ALPHA * ctx_h + cg1 * m_ctx, ln1_g[i], ln1_b[i])
            c_moe = _expert_choice(_modulate(ctx_h, csh2, csc2)[0], w_router[i], w_exp_gate[i], w_exp_up[i],
                                   w_exp_down[i])
            ctx_h = _layer_norm_affine(ALPHA * ctx_h + cg2 * c_moe[None], ln2_g[i], ln2_b[i])
    return x
```

```python
import functools
import math

import jax
import jax.numpy as jnp
from jax import lax
from jax.experimental import pallas as pl
from jax.experimental.pallas import tpu as pltpu

f32 = jnp.float32
bf16 = jnp.bfloat16
i32 = jnp.int32

D_MODEL = 2048
DEPTH = 4
GRID_W = 64
S5_WIDTH = 512
S5_GROUP_CH = 16
S5_GROUPS = 32
S5_STATE = 64
SSD_HEADDIM = 64
SSD_HEADS = 24
SSD_WIDTH = 1536
SSD_GROUPS = 4
SSD_HPG = 6
SSD_STATE = 128
SSD_BC = 512
SSD_CHUNK = 128
OFF_U = 0
OFF_X = 512
OFF_B = 2048
OFF_DT = 2560
OFF_C = 2608
OFF_Z = 3120
OFF_GATE_S5 = 4656
OFF_GATE_SSD = 6704
PROJ_WIDTH = 8752
CONV_CH = SSD_WIDTH + 2 * SSD_BC
N_EXPERTS = 16
CAPACITY_FACTOR = 2
N_MOD = 6
LN_EPS = 1e-5
ALPHA = (2 * DEPTH) ** 0.25

PW = 9216
VMEM_LIMIT = 56 * 1024 * 1024


def _cparams(sem):
    return pltpu.CompilerParams(dimension_semantics=sem, vmem_limit_bytes=VMEM_LIMIT)


def _dot(a, b):
    return jnp.dot(a, b, preferred_element_type=f32)


def _pick(n, cands):
    for c in cands:
        if n % c == 0:
            return c
    raise ValueError((n, cands))


def _ada_kernel(c_ref, w_ref, b_ref, o_ref):
    c = c_ref[...]
    s = c / (1.0 + jnp.exp(-c))
    hi = s.astype(bf16)
    lo = (s - hi.astype(f32)).astype(bf16)
    w = w_ref[...].astype(bf16)
    o_ref[...] = _dot(hi, w) + _dot(lo, w) + b_ref[...]


def ada_all(c8, w_ada, b_ada):
    n = N_MOD * D_MODEL
    tn = 1536
    return pl.pallas_call(
        _ada_kernel,
        grid=(DEPTH, n // tn),
        in_specs=[pl.BlockSpec((8, D_MODEL), lambda l, j: (0, 0)),
                  pl.BlockSpec((None, D_MODEL, tn), lambda l, j: (l, 0, j)),
                  pl.BlockSpec((None, 1, tn), lambda l, j: (l, 0, j))],
        out_specs=pl.BlockSpec((None, 8, tn), lambda l, j: (l, 0, j)),
        out_shape=jax.ShapeDtypeStruct((DEPTH, 8, n), f32),
        compiler_params=_cparams(("arbitrary", "arbitrary")),
        name="ada",
    )(c8, w_ada, b_ada.reshape(DEPTH, 1, n))


def _proj_kernel(x_ref, mod_ref, w_ref, o_ref, h_ref, *, tm, n_lat):
    i = pl.program_id(0)
    j = pl.program_id(1)

    @pl.when(j == 0)
    def _():
        x = x_ref[...]
        mu = jnp.mean(x, axis=-1, keepdims=True)
        xc = x - mu
        var = jnp.mean(xc * xc, axis=-1, keepdims=True)
        xn = xc * lax.rsqrt(var + LN_EPS)
        row = i * tm + lax.broadcasted_iota(i32, (tm, 1), 0)
        isctx = row >= n_lat
        shift = jnp.where(isctx, mod_ref[2:3, :], mod_ref[0:1, :])
        scale = jnp.where(isctx, mod_ref[3:4, :], mod_ref[1:2, :])
        h_ref[...] = (xn * (1.0 + scale) + shift).astype(bf16)

    o_ref[...] = _dot(h_ref[...], w_ref[...])


def proj(x_all, modrows, w_pad, n_lat):
    t = x_all.shape[0]
    tm = _pick(t, (1280, 768, 512, 256))
    tn = 1024
    return pl.pallas_call(
        functools.partial(_proj_kernel, tm=tm, n_lat=n_lat),
        grid=(t // tm, PW // tn),
        in_specs=[pl.BlockSpec((tm, D_MODEL), lambda i, j: (i, 0)),
                  pl.BlockSpec((8, D_MODEL), lambda i, j: (0, 0)),
                  pl.BlockSpec((D_MODEL, tn), lambda i, j: (0, j))],
        out_specs=pl.BlockSpec((tm, tn), lambda i, j: (i, j)),
        out_shape=jax.ShapeDtypeStruct((t, PW), f32),
        scratch_shapes=[pltpu.VMEM((tm, D_MODEL), bf16)],
        compiler_params=_cparams(("arbitrary", "arbitrary")),
        name="proj",
    )(x_all, modrows, w_pad)


def _mm_kernel(a_ref, b_ref, o_ref):
    o_ref[...] = _dot(a_ref[...].astype(bf16), b_ref[...].astype(bf16))


def matmul(a, b):
    m, k = a.shape
    n = b.shape[1]
    tm = _pick(m, (1024, 512, 256, 128, 32, 8))
    tn = _pick(n, (1024, 512, 256, 128))
    return pl.pallas_call(
        _mm_kernel,
        grid=(m // tm, n // tn),
        in_specs=[pl.BlockSpec((tm, k), lambda i, j: (i, 0)),
                  pl.BlockSpec((k, tn), lambda i, j: (0, j))],
        out_specs=pl.BlockSpec((tm, tn), lambda i, j: (i, j)),
        out_shape=jax.ShapeDtypeStruct((m, n), f32),
        compiler_params=_cparams(("arbitrary", "arbitrary")),
        name="mm",
    )(a, b)


def bmm(a, b):
    e, m, k = a.shape
    n = b.shape[2]
    tm = _pick(m, (1024, 512, 256, 128, 32, 8))
    tn = _pick(n, (1024, 512, 256, 128))
    return pl.pallas_call(
        _mm_kernel,
        grid=(e, m // tm, n // tn),
        in_specs=[pl.BlockSpec((None, tm, k), lambda x, i, j: (x, i, 0)),
                  pl.BlockSpec((None, k, tn), lambda x, i, j: (x, 0, j))],
        out_specs=pl.BlockSpec((None, tm, tn), lambda x, i, j: (x, i, j)),
        out_shape=jax.ShapeDtypeStruct((e, m, n), f32),
        compiler_params=_cparams(("arbitrary", "arbitrary", "arbitrary")),
        name="bmm",
    )(a, b)


def _layer_norm(x):
    mu = jnp.mean(x, axis=-1, keepdims=True)
    var = jnp.mean(jnp.square(x - mu), axis=-1, keepdims=True)
    return (x - mu) * lax.rsqrt(var + LN_EPS)


def _layer_norm_affine(x, g, b):
    return _layer_norm(x) * g + b


def _rms_norm(x, g):
    return x * lax.rsqrt(jnp.mean(jnp.square(x), axis=-1, keepdims=True) + LN_EPS) * g


def _modulate(x, shift, scale):
    return _layer_norm(x) * (1.0 + scale) + shift


def _flip(t, rev):
    return t[:, ::-1] if (rev and t is not None) else t


def _to_colmajor(t, rows):
    b, l, ch = t.shape
    return t.reshape(b, rows, GRID_W, ch).transpose(0, 2, 1, 3).reshape(b, l, ch)


def _from_colmajor(t, rows):
    b, l, ch = t.shape
    return t.reshape(b, GRID_W, rows, ch).transpose(0, 2, 1, 3).reshape(b, l, ch)


def _dwconv(x, w, b):
    pad = w.shape[0] // 2
    y = lax.conv_general_dilated(x, w[:, None, :].astype(x.dtype), window_strides=(1,),
                                 padding=[(pad, pad)], dimension_numbers=('NWC', 'WIO', 'NWC'),
                                 feature_group_count=x.shape[-1])
    return y + b


def _s5_glu(y, w_glu):
    y = jax.nn.gelu(y)
    bt, l, w = y.shape
    return y * jax.nn.sigmoid(matmul(y.reshape(bt * l, w), w_glu).reshape(bt, l, w))


S5_Q = 8
S5_LB = 128
S5_NB = S5_WIDTH // S5_LB
S5_SB = 2 * (S5_LB // S5_GROUP_CH) * S5_STATE
S5_TILE = 32


def _s5_tables(lam_re, lam_im, log_step, b_re, b_im, c_re, c_im):
    nb, ng, q = S5_NB, S5_LB // S5_GROUP_CH, S5_Q
    hp = lax.Precision.HIGHEST
    step = jnp.exp(log_step)[..., None]
    are = lam_re * step
    aim = lam_im * step

    def power(n):
        mag = jnp.exp(are * float(n))
        return mag * jnp.cos(aim * float(n)), mag * jnp.sin(aim * float(n))

    p1re, p1im = power(1)
    den = jnp.square(lam_re) + jnp.square(lam_im)
    qre = ((p1re - 1.0) * lam_re + p1im * lam_im) / den
    qim = (p1im * lam_re - (p1re - 1.0) * lam_im) / den
    bbre = qre[..., None] * b_re - qim[..., None] * b_im
    bbim = qre[..., None] * b_im + qim[..., None] * b_re
    bm, cm = [], []
    for n in range(q + 1):
        pre, pim = power(n)
        if n < q:
            bm.append(jnp.stack([pre[..., None] * bbre - pim[..., None] * bbim,
                                 pre[..., None] * bbim + pim[..., None] * bbre], axis=1))
        cm.append(jnp.stack([c_re * pre[:, :, None, :] - c_im * pim[:, :, None, :],
                             -(c_re * pim[:, :, None, :] + c_im * pre[:, :, None, :])], axis=1))
    bm = jnp.stack(bm, axis=1)
    cm = jnp.stack(cm, axis=1)
    eye = jnp.eye(ng, dtype=f32)
    bm8 = bm.reshape(2, q, 2, nb, ng, S5_STATE, S5_GROUP_CH)
    wan = jnp.einsum('dnrbgph,gG->dbnghrGp', bm8, eye).reshape(2, nb, q, S5_LB, S5_SB)
    cm8 = cm.reshape(2, q + 1, 2, nb, ng, S5_GROUP_CH, S5_STATE)
    wcn = jnp.einsum('dnrbGhp,gG->dbnrGpgh', cm8, eye).reshape(2, nb, q + 1, S5_SB, S5_LB)
    kd = jnp.einsum('dbjxk,dbky->dbjxy', wan, wcn[:, :, 0], precision=hp)
    zero = jnp.zeros((nb, S5_LB, S5_LB), f32)

    def toeplitz(d):
        rws = []
        for s in range(q):
            blk = []
            for t in range(q):
                j = (t - s) if d == 0 else (s - t)
                blk.append(kd[d, :, j] if j >= 0 else zero)
            rws.append(jnp.concatenate(blk, axis=-1))
        return jnp.concatenate(rws, axis=-2)

    w_t = jnp.stack([toeplitz(0), toeplitz(1)], axis=0).astype(bf16)
    wa_f = jnp.concatenate([wan[0, :, q - 1 - s] for s in range(q)], axis=-2)
    wa_r = jnp.concatenate([wan[1, :, s] for s in range(q)], axis=-2)
    w_a = jnp.stack([wa_f, wa_r], axis=0).astype(bf16)
    wc_f = jnp.concatenate([wcn[0, :, t + 1] for t in range(q)], axis=-1)
    wc_r = jnp.concatenate([wcn[1, :, q - t] for t in range(q)], axis=-1)
    w_c = jnp.stack([wc_f, wc_r], axis=0).astype(bf16)
    dps = []
    for m in range(1, q + 1):
        pre, pim = power(q * m)
        dps.append(jnp.concatenate([pre.reshape(2, nb, ng * S5_STATE), pim.reshape(2, nb, ng * S5_STATE)], axis=-1))
    dpb = jnp.stack(dps, axis=2)
    small = jnp.stack([dpb[:, :, 0], dpb[:, :, 1], dpb[:, :, 3]], axis=2)
    prow = jnp.stack([dpb[0], dpb[1, :, ::-1]], axis=0)
    tab = jnp.concatenate([small, jnp.zeros((2, nb, 5, S5_SB), f32), prow], axis=2)
    return w_a, w_t, w_c, tab


def _u_block(us, bb):
    return jnp.concatenate([u[:, bb * S5_LB:(bb + 1) * S5_LB] for u in us], axis=1)


def _s5a_kernel(*refs):
    u_refs = refs[:S5_Q]
    wa_ref, s_ref = refs[S5_Q:]
    b = pl.program_id(2)
    for bb in range(S5_NB):
        @pl.when(b == bb)
        def _(bb=bb):
            ub = _u_block([u[...] for u in u_refs], bb).astype(bf16)
            s_ref[...] = _dot(ub, wa_ref[...])


def _u_specs(tk):
    blk = PW // S5_WIDTH
    off = OFF_U // S5_WIDTH
    return [pl.BlockSpec((tk, S5_WIDTH), (lambda c, d, b, s=s: (c, s * blk + off))) for s in range(S5_Q)]


def _s5_pass_a(p8, w_a):
    nc = p8.shape[0]
    tk = _pick(nc, (208, 264, 176, 96, 48, 32))
    return pl.pallas_call(
        _s5a_kernel,
        grid=(nc // tk, 2, S5_NB),
        in_specs=_u_specs(tk) + [pl.BlockSpec((None, None, S5_SB, S5_SB), lambda c, d, b: (d, b, 0, 0))],
        out_specs=pl.BlockSpec((None, None, tk, S5_SB), lambda c, d, b: (d, b, c, 0)),
        out_shape=jax.ShapeDtypeStruct((2, S5_NB, nc, S5_SB), f32),
        compiler_params=_cparams(("arbitrary", "arbitrary", "arbitrary")),
        name="s5_a",
    )(*([p8] * S5_Q), w_a)


def _cmul(are, aim, bre, bim):
    return are * bre - aim * bim, are * bim + aim * bre


def _s5_scan_tile(sre, sim, cre, cim, tab, rev):
    half = S5_SB // 2
    row = lax.broadcasted_iota(i32, (8, half), 0)
    are, aim = sre, sim
    for lvl, k in enumerate((1, 2, 4)):
        dre = tab[lvl:lvl + 1, :half]
        dim = tab[lvl:lvl + 1, half:]
        if rev:
            keep = row < 8 - k
            shre = jnp.where(keep, pltpu.roll(are, 8 - k, 0), 0.0)
            shim = jnp.where(keep, pltpu.roll(aim, 8 - k, 0), 0.0)
        else:
            keep = row >= k
            shre = jnp.where(keep, pltpu.roll(are, k, 0), 0.0)
            shim = jnp.where(keep, pltpu.roll(aim, k, 0), 0.0)
        mre, mim = _cmul(dre, dim, shre, shim)
        are = are + mre
        aim = aim + mim
    mre, mim = _cmul(tab[8:16, :half], tab[8:16, half:], cre, cim)
    ore = are + mre
    oim = aim + mim
    if rev:
        xre = jnp.where(row < 7, pltpu.roll(ore, 7, 0), cre)
        xim = jnp.where(row < 7, pltpu.roll(oim, 7, 0), cim)
        return xre, xim, ore[0:1], oim[0:1]
    xre = jnp.where(row >= 1, pltpu.roll(ore, 1, 0), cre)
    xim = jnp.where(row >= 1, pltpu.roll(oim, 1, 0), cim)
    return xre, xim, ore[7:8], oim[7:8]


def _s5b_kernel(sf_ref, sr_ref, tab_ref, xf_ref, xr_ref, carry_ref):
    half = S5_SB // 2

    @pl.when(pl.program_id(0) == 0)
    def _():
        carry_ref[...] = jnp.zeros_like(carry_ref)

    for d, (s_ref, x_ref) in enumerate(((sf_ref, xf_ref), (sr_ref, xr_ref))):
        for b in range(S5_NB):
            tab = tab_ref[d, b]
            k = d * S5_NB + b
            c = carry_ref[k:k + 1, :]
            cre, cim = c[:, :half], c[:, half:]
            order = list(range(S5_TILE // 8))
            for rt in (order[::-1] if d else order):
                tile = s_ref[b, rt * 8:(rt + 1) * 8, :]
                xre, xim, cre, cim = _s5_scan_tile(tile[:, :half], tile[:, half:], cre, cim, tab, bool(d))
                x_ref[b, rt * 8:(rt + 1) * 8, :] = jnp.concatenate([xre, xim], axis=1)
            carry_ref[k:k + 1, :] = jnp.concatenate([cre, cim], axis=1)


def _s5_pass_b(s_all, tab, n_lat_tiles):
    nc = s_all.shape[2]
    nt = nc // S5_TILE
    blk = (None, S5_NB, S5_TILE, S5_SB)
    oblk = (S5_NB, S5_TILE, S5_SB)
    return pl.pallas_call(
        _s5b_kernel,
        grid=(nt,),
        in_specs=[pl.BlockSpec(blk, lambda j: (0, 0, (j + n_lat_tiles) % nt, 0)),
                  pl.BlockSpec(blk, lambda j: (1, 0, nt - 1 - j, 0)),
                  pl.BlockSpec((2, S5_NB, 16, S5_SB), lambda j: (0, 0, 0, 0))],
        out_specs=[pl.BlockSpec(oblk, lambda j: (0, (j + n_lat_tiles) % nt, 0)),
                   pl.BlockSpec(oblk, lambda j: (0, nt - 1 - j, 0))],
        out_shape=[jax.ShapeDtypeStruct((S5_NB, nc, S5_SB), f32)] * 2,
        scratch_shapes=[pltpu.VMEM((2 * S5_NB, S5_SB), f32)],
        compiler_params=_cparams(("arbitrary",)),
        name="s5_b",
    )(s_all, s_all, tab)


def _s5c_kernel(*refs):
    u_refs = refs[:S5_Q]
    xf_ref, xr_ref, wt_ref, wc_ref, dsk_ref, y_ref = refs[S5_Q:]
    d = pl.program_id(1)
    b = pl.program_id(2)

    @pl.when((d == 0) & (b == 0))
    def _():
        for s, u in enumerate(u_refs):
            y_ref[:, s * S5_WIDTH:(s + 1) * S5_WIDTH] = u[...] * dsk_ref[...]

    for bb in range(S5_NB):
        @pl.when(b == bb)
        def _(bb=bb):
            ub = _u_block([u[...] for u in u_refs], bb).astype(bf16)
            xin = jnp.where(d == 0, xf_ref[bb], xr_ref[bb]).astype(bf16)
            yb = _dot(ub, wt_ref[...]) + _dot(xin, wc_ref[...])
            for s in range(S5_Q):
                lo = s * S5_WIDTH + bb * S5_LB
                y_ref[:, lo:lo + S5_LB] += yb[:, s * S5_LB:(s + 1) * S5_LB]


def _s5_pass_c(p8, xf, xr, w_t, w_c, dsk):
    nc = p8.shape[0]
    tk = _pick(nc, (208, 264, 176, 96, 48, 32))
    wspec = pl.BlockSpec((None, None, S5_SB, S5_SB), lambda c, d, b: (d, b, 0, 0))
    xspec = pl.BlockSpec((S5_NB, tk, S5_SB), lambda c, d, b: (0, c, 0))
    return pl.pallas_call(
        _s5c_kernel,
        grid=(nc // tk, 2, S5_NB),
        in_specs=_u_specs(tk) + [xspec, xspec, wspec, wspec,
                                 pl.BlockSpec((1, S5_WIDTH), lambda c, d, b: (0, 0))],
        out_specs=pl.BlockSpec((tk, S5_Q * S5_WIDTH), lambda c, d, b: (c, 0)),
        out_shape=jax.ShapeDtypeStruct((nc, S5_Q * S5_WIDTH), f32),
        compiler_params=_cparams(("arbitrary", "arbitrary", "arbitrary")),
        name="s5_c",
    )(*([p8] * S5_Q), xf, xr, w_t, w_c, dsk)


def s5_mix(p_all, tables, dsk, n_lat):
    w_a, w_t, w_c, tab = tables
    t = p_all.shape[0]
    p8 = p_all.reshape(t // S5_Q, S5_Q * PW)
    s_all = _s5_pass_a(p8, w_a)
    xf, xr = _s5_pass_b(s_all, tab, n_lat // (S5_Q * S5_TILE))
    y8 = _s5_pass_c(p8, xf, xr, w_t, w_c, dsk.reshape(1, S5_WIDTH))
    return y8.reshape(t, S5_WIDTH)


def _ssd_scan(x, dt, a, b, c, s0):
    bt, seq, g, r, p = x.shape
    n = b.shape[-1]
    nc = seq // SSD_CHUNK
    xc = x.reshape(bt, nc, SSD_CHUNK, g, r, p)
    dtc = dt.reshape(bt, nc, SSD_CHUNK, g, r)
    bc = b.reshape(bt, nc, SSD_CHUNK, g, n)
    acum = jnp.cumsum(dtc * a, axis=2)
    a_last = acum[:, :, -1]
    w_state = jnp.exp(a_last[:, :, None] - acum) * dtc
    chunk_states = jnp.einsum('bcsgn,bcsgr,bcsgrp->bcgrpn', bc, w_state, xc)

    def step(s, inp):
        st, decay = inp
        return decay[..., None, None] * s + st, s

    s_final, s_in = lax.scan(step, s0, (jnp.moveaxis(chunk_states, 1, 0), jnp.moveaxis(jnp.exp(a_last), 1, 0)))
    if c is None:
        return None, s_final
    s_in = jnp.moveaxis(s_in, 0, 1)
    cc = c.reshape(bt, nc, SSD_CHUNK, g, n)
    seg = acum[:, :, :, None] - acum[:, :, None]
    lower = jnp.tril(jnp.ones((SSD_CHUNK, SSD_CHUNK), dtype=bool))[:, :, None, None]
    decay = jnp.exp(jnp.where(lower, seg, -jnp.inf))
    cb = jnp.einsum('bcqgn,bcsgn->bcqsg', cc, bc)
    y_diag = jnp.einsum('bcqsgr,bcsgrp->bcqgrp', cb[..., None] * decay * dtc[:, :, None], xc)
    y_off = jnp.einsum('bcqgn,bcgrpn->bcqgrp', cc, s_in) * jnp.exp(acum)[..., None]
    return (y_diag + y_off).reshape(bt, seq, g, r, p), s_final


def _split_xbc(xbc):
    bt, l, ch = xbc.shape
    xs = xbc[..., :SSD_WIDTH].reshape(bt, l, SSD_GROUPS, SSD_HPG, SSD_HEADDIM)
    bs = xbc[..., SSD_WIDTH:SSD_WIDTH + SSD_BC].reshape(bt, l, SSD_GROUPS, SSD_STATE)
    cs = xbc[..., SSD_WIDTH + SSD_BC:].reshape(bt, l, SSD_GROUPS, SSD_STATE) if ch == CONV_CH else None
    return xs, bs, cs


def _ssd_branch(p_lat, p_ctx, rows, conv_w, conv_b, a_log, dt_bias, d, norm_w, ctx_out):
    bt, seq, _ = p_lat.shape
    n_ctx = p_ctx.shape[1]
    xb_w = OFF_DT - OFF_X
    s_lat = _to_colmajor(p_lat[..., OFF_X:OFF_Z], rows)
    xbc_lat = jnp.concatenate([s_lat[..., :xb_w], s_lat[..., OFF_C - OFF_X:]], axis=-1)
    dt_lat = s_lat[..., xb_w:OFF_C - OFF_X]
    xbc_lat = jax.nn.silu(_dwconv(xbc_lat, conv_w, conv_b))
    if ctx_out:
        xbc_ctx = jnp.concatenate([p_ctx[..., OFF_X:OFF_DT], p_ctx[..., OFF_C:OFF_Z]], axis=-1)
        xbc_ctx = jax.nn.silu(_dwconv(xbc_ctx, conv_w, conv_b))
    else:
        xbc_ctx = jax.nn.silu(_dwconv(p_ctx[..., OFF_X:OFF_DT], conv_w[:, :xb_w], conv_b[:xb_w]))
    dt_ctx = p_ctx[..., OFF_DT:OFF_C]
    x_l, b_l, c_l = _split_xbc(xbc_lat)
    x_c, b_c, c_c = _split_xbc(xbc_ctx)
    d_skip = d.reshape(SSD_GROUPS, SSD_HPG, 1)
    y_lat = x_l * d_skip
    y_ctx = x_c * d_skip if ctx_out else None
    zero = jnp.zeros((bt, SSD_GROUPS, SSD_HPG, SSD_HEADDIM, SSD_STATE), f32)
    for k, rev in enumerate((False, True)):
        a = -jnp.exp(a_log[k]).reshape(SSD_GROUPS, SSD_HPG)
        dtb = dt_bias[k]
        lo, hi = k * SSD_HEADS, (k + 1) * SSD_HEADS
        dt_l = jax.nn.softplus(dt_lat[..., lo:hi] + dtb).reshape(bt, seq, SSD_GROUPS, SSD_HPG)
        dt_c = jax.nn.softplus(dt_ctx[..., lo:hi] + dtb).reshape(bt, n_ctx, SSD_GROUPS, SSD_HPG)
        yc, s_ctx = _ssd_scan(_flip(x_c, rev), _flip(dt_c, rev), a, _flip(b_c, rev), _flip(c_c, rev), zero)
        yl, _ = _ssd_scan(_flip(x_l, rev), _flip(dt_l, rev), a, _flip(b_l, rev), _flip(c_l, rev), s_ctx)
        y_lat = y_lat + _flip(yl, rev)
        if ctx_out:
            y_ctx = y_ctx + _flip(yc, rev)
    y_lat = _from_colmajor(y_lat.reshape(bt, seq, SSD_WIDTH), rows)
    out_lat = _rms_norm(y_lat * jax.nn.silu(p_lat[..., OFF_Z:OFF_GATE_S5]), norm_w)
    if not ctx_out:
        return out_lat, None
    y_ctx = y_ctx.reshape(bt, n_ctx, SSD_WIDTH)
    out_ctx = _rms_norm(y_ctx * jax.nn.silu(p_ctx[..., OFF_Z:OFF_GATE_S5]), norm_w)
    return out_lat, out_ctx


def _merge(p, ya, yb, w_branch_s5, w_branch_ssd, w_out):
    bt, l, _ = p.shape
    g_a = jax.nn.sigmoid(p[..., OFF_GATE_S5:OFF_GATE_SSD]).reshape(bt * l, D_MODEL)
    g_b = jax.nn.sigmoid(p[..., OFF_GATE_SSD:PROJ_WIDTH]).reshape(bt * l, D_MODEL)
    za = matmul(ya.reshape(bt * l, S5_WIDTH), w_branch_s5)
    zb = matmul(yb.reshape(bt * l, SSD_WIDTH), w_branch_ssd)
    return matmul(g_a * za + g_b * zb, w_out).reshape(bt, l, D_MODEL)


def _expert_choice(h, w_router, w_gate, w_up, w_down):
    n_tok = h.shape[0]
    cap = CAPACITY_FACTOR * n_tok // N_EXPERTS
    aff = jax.nn.softmax((h @ w_router).astype(f32), axis=-1)
    g, idx = lax.top_k(aff.T, cap)
    xe = h[idx]
    hid = jax.nn.silu(bmm(xe, w_gate)) * bmm(xe, w_up)
    ye = bmm(hid, w_down) * g[..., None]
    return jnp.zeros_like(h).at[idx.reshape(-1)].add(ye.reshape(-1, h.shape[-1]))


def kernel(x, c, ctx, c_ctx, w_ada, b_ada, w_in, s5_lambda_re, s5_lambda_im, s5_log_step, s5_b_re, s5_b_im,
           s5_c_re, s5_c_im, s5_d, s5_w_glu, ssd_conv_w, ssd_conv_b, ssd_a_log, ssd_dt_bias, ssd_d, ssd_norm_w,
           w_branch_s5, w_branch_ssd, w_out, ln1_g, ln1_b, w_router, w_exp_gate, w_exp_up, w_exp_down,
           ln2_g, ln2_b):
    bt, seq, _ = x.shape
    n_ctx = ctx.shape[1]
    rows = seq // GRID_W
    c8 = jnp.zeros((8, D_MODEL), f32).at[0].set(c[0]).at[1].set(c_ctx)
    mods = ada_all(c8, w_ada, b_ada)
    w_pad = jnp.pad(w_in, ((0, 0), (0, 0), (0, PW - PROJ_WIDTH))).astype(bf16)
    ctx_h = ctx
    for i in range(DEPTH):
        ctx_out = i < DEPTH - 1
        sh1, sc1, g1, sh2, sc2, g2 = jnp.split(mods[i, 0], N_MOD)
        csh1, csc1, cg1, csh2, csc2, cg2 = jnp.split(mods[i, 1], N_MOD)
        modrows = jnp.zeros((8, D_MODEL), f32).at[0].set(sh1).at[1].set(sc1).at[2].set(csh1).at[3].set(csc1)
        x_all = jnp.concatenate([x[0], ctx_h[0]], axis=0)
        p_all = proj(x_all, modrows, w_pad[i], seq)
        p_lat = p_all[None, :seq, :PROJ_WIDTH]
        p_ctx = p_all[None, seq:, :PROJ_WIDTH]
        s5_tab = _s5_tables(s5_lambda_re[i], s5_lambda_im[i], s5_log_step[i], s5_b_re[i], s5_b_im[i],
                            s5_c_re[i], s5_c_im[i])
        y_s5 = s5_mix(p_all, s5_tab, s5_d[i], seq)
        a_lat = _s5_glu(y_s5[None, :seq], s5_w_glu[i])
        a_ctx = _s5_glu(y_s5[None, seq:], s5_w_glu[i]) if ctx_out else None
        b_lat, b_ctx = _ssd_branch(p_lat, p_ctx, rows, ssd_conv_w[i], ssd_conv_b[i], ssd_a_log[i], ssd_dt_bias[i],
                                   ssd_d[i], ssd_norm_w[i], ctx_out)
        m_lat = _merge(p_lat, a_lat, b_lat, w_branch_s5[i], w_branch_ssd[i], w_out[i])
        x = _layer_norm_affine(ALPHA * x + g1 * m_lat, ln1_g[i], ln1_b[i])
        y_moe = _expert_choice(_modulate(x, sh2, sc2)[0], w_router[i], w_exp_gate[i], w_exp_up[i], w_exp_down[i])
        x = _layer_norm_affine(ALPHA * x + g2 * y_moe[None], ln2_g[i], ln2_b[i])
        if ctx_out:
            m_ctx = _merge(p_ctx, a_ctx, b_ctx, w_branch_s5[i], w_branch_ssd[i], w_out[i])
            ctx_h = _layer_norm_affine(ALPHA * ctx_h + cg1 * m_ctx, ln1_g[i], ln1_b[i])
            c_moe = _expert_choice(_modulate(ctx_h, csh2, csc2)[0], w_router[i], w_exp_gate[i], w_exp_up[i],
                                   w_exp_down[i])
            ctx_h = _layer_norm_affine(ALPHA * ctx_h + cg2 * c_moe[None], ln2_g[i], ln2_b[i])
    return x
```

```python
import functools
import math

import jax
import jax.numpy as jnp
from jax import lax
from jax.experimental import pallas as pl
from jax.experimental.pallas import tpu as pltpu

f32 = jnp.float32
bf16 = jnp.bfloat16
i32 = jnp.int32

D_MODEL = 2048
DEPTH = 4
GRID_W = 64
S5_WIDTH = 512
S5_GROUP_CH = 16
S5_GROUPS = 32
S5_STATE = 64
SSD_HEADDIM = 64
SSD_HEADS = 24
SSD_WIDTH = 1536
SSD_GROUPS = 4
SSD_HPG = 6
SSD_STATE = 128
SSD_BC = 512
SSD_CHUNK = 128
OFF_U = 0
OFF_X = 512
OFF_B = 2048
OFF_DT = 2560
OFF_C = 2608
OFF_Z = 3120
OFF_GATE_S5 = 4656
OFF_GATE_SSD = 6704
PROJ_WIDTH = 8752
CONV_CH = SSD_WIDTH + 2 * SSD_BC
N_EXPERTS = 16
CAPACITY_FACTOR = 2
N_MOD = 6
LN_EPS = 1e-5
ALPHA = (2 * DEPTH) ** 0.25

PW = 9216
VMEM_LIMIT = 56 * 1024 * 1024


def _cparams(sem):
    return pltpu.CompilerParams(dimension_semantics=sem, vmem_limit_bytes=VMEM_LIMIT)


def _dot(a, b):
    return jnp.dot(a, b, preferred_element_type=f32)


def _pick(n, cands):
    for c in cands:
        if n % c == 0:
            return c
    raise ValueError((n, cands))


def _ada_kernel(c_ref, w_ref, b_ref, o_ref):
    c = c_ref[...]
    s = c / (1.0 + jnp.exp(-c))
    hi = s.astype(bf16)
    lo = (s - hi.astype(f32)).astype(bf16)
    w = w_ref[...].astype(bf16)
    o_ref[...] = _dot(hi, w) + _dot(lo, w) + b_ref[...]


def ada_all(c8, w_ada, b_ada):
    n = N_MOD * D_MODEL
    tn = 1536
    return pl.pallas_call(
        _ada_kernel,
        grid=(DEPTH, n // tn),
        in_specs=[pl.BlockSpec((8, D_MODEL), lambda l, j: (0, 0)),
                  pl.BlockSpec((None, D_MODEL, tn), lambda l, j: (l, 0, j)),
                  pl.BlockSpec((None, 1, tn), lambda l, j: (l, 0, j))],
        out_specs=pl.BlockSpec((None, 8, tn), lambda l, j: (l, 0, j)),
        out_shape=jax.ShapeDtypeStruct((DEPTH, 8, n), f32),
        compiler_params=_cparams(("arbitrary", "arbitrary")),
        name="ada",
    )(c8, w_ada, b_ada.reshape(DEPTH, 1, n))


def _proj_kernel(x_ref, mod_ref, w_ref, o_ref, h_ref, *, tm, n_lat):
    i = pl.program_id(0)
    j = pl.program_id(1)

    @pl.when(j == 0)
    def _():
        x = x_ref[...]
        mu = jnp.mean(x, axis=-1, keepdims=True)
        xc = x - mu
        var = jnp.mean(xc * xc, axis=-1, keepdims=True)
        xn = xc * lax.rsqrt(var + LN_EPS)
        row = i * tm + lax.broadcasted_iota(i32, (tm, 1), 0)
        isctx = row >= n_lat
        shift = jnp.where(isctx, mod_ref[2:3, :], mod_ref[0:1, :])
        scale = jnp.where(isctx, mod_ref[3:4, :], mod_ref[1:2, :])
        h_ref[...] = (xn * (1.0 + scale) + shift).astype(bf16)

    o_ref[...] = _dot(h_ref[...], w_ref[...])


def proj(x_all, modrows, w_pad, n_lat):
    t = x_all.shape[0]
    tm = _pick(t, (1280, 768, 512, 256))
    tn = 1024
    return pl.pallas_call(
        functools.partial(_proj_kernel, tm=tm, n_lat=n_lat),
        grid=(t // tm, PW // tn),
        in_specs=[pl.BlockSpec((tm, D_MODEL), lambda i, j: (i, 0)),
                  pl.BlockSpec((8, D_MODEL), lambda i, j: (0, 0)),
                  pl.BlockSpec((D_MODEL, tn), lambda i, j: (0, j))],
        out_specs=pl.BlockSpec((tm, tn), lambda i, j: (i, j)),
        out_shape=jax.ShapeDtypeStruct((t, PW), f32),
        scratch_shapes=[pltpu.VMEM((tm, D_MODEL), bf16)],
        compiler_params=_cparams(("arbitrary", "arbitrary")),
        name="proj",
    )(x_all, modrows, w_pad)


def _mm_kernel(a_ref, b_ref, o_ref):
    o_ref[...] = _dot(a_ref[...].astype(bf16), b_ref[...].astype(bf16))


def matmul(a, b):
    m, k = a.shape
    n = b.shape[1]
    tm = _pick(m, (1024, 512, 256, 128, 32, 8))
    tn = _pick(n, (1024, 512, 256, 128))
    return pl.pallas_call(
        _mm_kernel,
        grid=(m // tm, n // tn),
        in_specs=[pl.BlockSpec((tm, k), lambda i, j: (i, 0)),
                  pl.BlockSpec((k, tn), lambda i, j: (0, j))],
        out_specs=pl.BlockSpec((tm, tn), lambda i, j: (i, j)),
        out_shape=jax.ShapeDtypeStruct((m, n), f32),
        compiler_params=_cparams(("arbitrary", "arbitrary")),
        name="mm",
    )(a, b)


def bmm(a, b):
    e, m, k = a.shape
    n = b.shape[2]
    tm = _pick(m, (1024, 512, 256, 128, 32, 8))
    tn = _pick(n, (1024, 512, 256, 128))
    return pl.pallas_call(
        _mm_kernel,
        grid=(e, m // tm, n // tn),
        in_specs=[pl.BlockSpec((None, tm, k), lambda x, i, j: (x, i, 0)),
                  pl.BlockSpec((None, k, tn), lambda x, i, j: (x, 0, j))],
        out_specs=pl.BlockSpec((None, tm, tn), lambda x, i, j: (x, i, j)),
        out_shape=jax.ShapeDtypeStruct((e, m, n), f32),
        compiler_params=_cparams(("arbitrary", "arbitrary", "arbitrary")),
        name="bmm",
    )(a, b)


def _swiglu_kernel(a_ref, g_ref, u_ref, o_ref):
    a = a_ref[...].astype(bf16)
    g = _dot(a, g_ref[...].astype(bf16))
    u = _dot(a, u_ref[...].astype(bf16))
    o_ref[...] = (g / (1.0 + jnp.exp(-g)) * u).astype(o_ref.dtype)


def swiglu_bmm(a, w_gate, w_up):
    e, m, k = a.shape
    n = w_gate.shape[2]
    tm = _pick(m, (1024, 512, 256, 128, 32, 16))
    tn = _pick(n, (512, 256, 128))
    wspec = pl.BlockSpec((None, k, tn), lambda x, i, j: (x, 0, j))
    return pl.pallas_call(
        _swiglu_kernel,
        grid=(e, m // tm, n // tn),
        in_specs=[pl.BlockSpec((None, tm, k), lambda x, i, j: (x, i, 0)), wspec, wspec],
        out_specs=pl.BlockSpec((None, tm, tn), lambda x, i, j: (x, i, j)),
        out_shape=jax.ShapeDtypeStruct((e, m, n), bf16),
        compiler_params=_cparams(("arbitrary", "arbitrary", "arbitrary")),
        name="swiglu_bmm",
    )(a, w_gate, w_up)


def _layer_norm(x):
    mu = jnp.mean(x, axis=-1, keepdims=True)
    var = jnp.mean(jnp.square(x - mu), axis=-1, keepdims=True)
    return (x - mu) * lax.rsqrt(var + LN_EPS)


def _layer_norm_affine(x, g, b):
    return _layer_norm(x) * g + b


def _rms_norm(x, g):
    return x * lax.rsqrt(jnp.mean(jnp.square(x), axis=-1, keepdims=True) + LN_EPS) * g


def _modulate(x, shift, scale):
    return _layer_norm(x) * (1.0 + scale) + shift


def _flip(t, rev):
    return t[:, ::-1] if (rev and t is not None) else t


def _to_colmajor(t, rows):
    b, l, ch = t.shape
    return t.reshape(b, rows, GRID_W, ch).transpose(0, 2, 1, 3).reshape(b, l, ch)


def _from_colmajor(t, rows):
    b, l, ch = t.shape
    return t.reshape(b, GRID_W, rows, ch).transpose(0, 2, 1, 3).reshape(b, l, ch)


def _dwconv(x, w, b):
    pad = w.shape[0] // 2
    y = lax.conv_general_dilated(x, w[:, None, :].astype(x.dtype), window_strides=(1,),
                                 padding=[(pad, pad)], dimension_numbers=('NWC', 'WIO', 'NWC'),
                                 feature_group_count=x.shape[-1])
    return y + b


def _s5_glu(y, w_glu):
    y = jax.nn.gelu(y)
    bt, l, w = y.shape
    return y * jax.nn.sigmoid(matmul(y.reshape(bt * l, w), w_glu).reshape(bt, l, w))


S5_Q = 8
S5_LB = 128
S5_NB = S5_WIDTH // S5_LB
S5_SB = 2 * (S5_LB // S5_GROUP_CH) * S5_STATE
S5_TILE = 32


def _s5_tables(lam_re, lam_im, log_step, b_re, b_im, c_re, c_im):
    nb, ng, q = S5_NB, S5_LB // S5_GROUP_CH, S5_Q
    hp = lax.Precision.HIGHEST
    step = jnp.exp(log_step)[..., None]
    are = lam_re * step
    aim = lam_im * step

    def power(n):
        mag = jnp.exp(are * float(n))
        return mag * jnp.cos(aim * float(n)), mag * jnp.sin(aim * float(n))

    p1re, p1im = power(1)
    den = jnp.square(lam_re) + jnp.square(lam_im)
    qre = ((p1re - 1.0) * lam_re + p1im * lam_im) / den
    qim = (p1im * lam_re - (p1re - 1.0) * lam_im) / den
    bbre = qre[..., None] * b_re - qim[..., None] * b_im
    bbim = qre[..., None] * b_im + qim[..., None] * b_re
    bm, cm = [], []
    for n in range(q + 1):
        pre, pim = power(n)
        if n < q:
            bm.append(jnp.stack([pre[..., None] * bbre - pim[..., None] * bbim,
                                 pre[..., None] * bbim + pim[..., None] * bbre], axis=1))
        cm.append(jnp.stack([c_re * pre[:, :, None, :] - c_im * pim[:, :, None, :],
                             -(c_re * pim[:, :, None, :] + c_im * pre[:, :, None, :])], axis=1))
    bm = jnp.stack(bm, axis=1)
    cm = jnp.stack(cm, axis=1)
    eye = jnp.eye(ng, dtype=f32)
    bm8 = bm.reshape(2, q, 2, nb, ng, S5_STATE, S5_GROUP_CH)
    wan = jnp.einsum('dnrbgph,gG->dbnghrGp', bm8, eye).reshape(2, nb, q, S5_LB, S5_SB)
    cm8 = cm.reshape(2, q + 1, 2, nb, ng, S5_GROUP_CH, S5_STATE)
    wcn = jnp.einsum('dnrbGhp,gG->dbnrGpgh', cm8, eye).reshape(2, nb, q + 1, S5_SB, S5_LB)
    kd = jnp.einsum('dbjxk,dbky->dbjxy', wan, wcn[:, :, 0], precision=hp)
    zero = jnp.zeros((nb, S5_LB, S5_LB), f32)

    def toeplitz(d):
        rws = []
        for s in range(q):
            blk = []
            for t in range(q):
                j = (t - s) if d == 0 else (s - t)
                blk.append(kd[d, :, j] if j >= 0 else zero)
            rws.append(jnp.concatenate(blk, axis=-1))
        return jnp.concatenate(rws, axis=-2)

    w_t = jnp.stack([toeplitz(0), toeplitz(1)], axis=0).astype(bf16)
    wa_f = jnp.concatenate([wan[0, :, q - 1 - s] for s in range(q)], axis=-2)
    wa_r = jnp.concatenate([wan[1, :, s] for s in range(q)], axis=-2)
    w_a = jnp.stack([wa_f, wa_r], axis=0).astype(bf16)
    wc_f = jnp.concatenate([wcn[0, :, t + 1] for t in range(q)], axis=-1)
    wc_r = jnp.concatenate([wcn[1, :, q - t] for t in range(q)], axis=-1)
    w_c = jnp.stack([wc_f, wc_r], axis=0).astype(bf16)
    dps = []
    for m in range(1, q + 1):
        pre, pim = power(q * m)
        dps.append(jnp.concatenate([pre.reshape(2, nb, ng * S5_STATE), pim.reshape(2, nb, ng * S5_STATE)], axis=-1))
    dpb = jnp.stack(dps, axis=2)
    small = jnp.stack([dpb[:, :, 0], dpb[:, :, 1], dpb[:, :, 3]], axis=2)
    prow = jnp.stack([dpb[0], dpb[1, :, ::-1]], axis=0)
    tab = jnp.concatenate([small, jnp.zeros((2, nb, 5, S5_SB), f32), prow], axis=2)
    return w_a, w_t, w_c, tab


def _u_block(us, bb):
    return jnp.concatenate([u[:, bb * S5_LB:(bb + 1) * S5_LB] for u in us], axis=1)


def _s5a_kernel(*refs):
    u_refs = refs[:S5_Q]
    wa_ref, s_ref = refs[S5_Q:]
    b = pl.program_id(2)
    for bb in range(S5_NB):
        @pl.when(b == bb)
        def _(bb=bb):
            ub = _u_block([u[...] for u in u_refs], bb).astype(bf16)
            s_ref[...] = _dot(ub, wa_ref[...])


def _u_specs(tk):
    blk = PW // S5_WIDTH
    off = OFF_U // S5_WIDTH
    return [pl.BlockSpec((tk, S5_WIDTH), (lambda c, d, b, s=s: (c, s * blk + off))) for s in range(S5_Q)]


def _s5_pass_a(p8, w_a):
    nc = p8.shape[0]
    tk = _pick(nc, (208, 264, 176, 96, 48, 32))
    return pl.pallas_call(
        _s5a_kernel,
        grid=(nc // tk, 2, S5_NB),
        in_specs=_u_specs(tk) + [pl.BlockSpec((None, None, S5_SB, S5_SB), lambda c, d, b: (d, b, 0, 0))],
        out_specs=pl.BlockSpec((None, None, tk, S5_SB), lambda c, d, b: (d, b, c, 0)),
        out_shape=jax.ShapeDtypeStruct((2, S5_NB, nc, S5_SB), f32),
        compiler_params=_cparams(("arbitrary", "arbitrary", "arbitrary")),
        name="s5_a",
    )(*([p8] * S5_Q), w_a)


def _cmul(are, aim, bre, bim):
    return are * bre - aim * bim, are * bim + aim * bre


def _s5_scan_tile(sre, sim, cre, cim, tab, rev):
    half = S5_SB // 2
    row = lax.broadcasted_iota(i32, (8, half), 0)
    are, aim = sre, sim
    for lvl, k in enumerate((1, 2, 4)):
        dre = tab[lvl:lvl + 1, :half]
        dim = tab[lvl:lvl + 1, half:]
        if rev:
            keep = row < 8 - k
            shre = jnp.where(keep, pltpu.roll(are, 8 - k, 0), 0.0)
            shim = jnp.where(keep, pltpu.roll(aim, 8 - k, 0), 0.0)
        else:
            keep = row >= k
            shre = jnp.where(keep, pltpu.roll(are, k, 0), 0.0)
            shim = jnp.where(keep, pltpu.roll(aim, k, 0), 0.0)
        mre, mim = _cmul(dre, dim, shre, shim)
        are = are + mre
        aim = aim + mim
    mre, mim = _cmul(tab[8:16, :half], tab[8:16, half:], cre, cim)
    ore = are + mre
    oim = aim + mim
    if rev:
        xre = jnp.where(row < 7, pltpu.roll(ore, 7, 0), cre)
        xim = jnp.where(row < 7, pltpu.roll(oim, 7, 0), cim)
        return xre, xim, ore[0:1], oim[0:1]
    xre = jnp.where(row >= 1, pltpu.roll(ore, 1, 0), cre)
    xim = jnp.where(row >= 1, pltpu.roll(oim, 1, 0), cim)
    return xre, xim, ore[7:8], oim[7:8]


def _s5b_kernel(sf_ref, sr_ref, tab_ref, xf_ref, xr_ref, carry_ref):
    half = S5_SB // 2

    @pl.when(pl.program_id(0) == 0)
    def _():
        carry_ref[...] = jnp.zeros_like(carry_ref)

    for d, (s_ref, x_ref) in enumerate(((sf_ref, xf_ref), (sr_ref, xr_ref))):
        for b in range(S5_NB):
            tab = tab_ref[d, b]
            k = d * S5_NB + b
            c = carry_ref[k:k + 1, :]
            cre, cim = c[:, :half], c[:, half:]
            order = list(range(S5_TILE // 8))
            for rt in (order[::-1] if d else order):
                tile = s_ref[b, rt * 8:(rt + 1) * 8, :]
                xre, xim, cre, cim = _s5_scan_tile(tile[:, :half], tile[:, half:], cre, cim, tab, bool(d))
                x_ref[b, rt * 8:(rt + 1) * 8, :] = jnp.concatenate([xre, xim], axis=1)
            carry_ref[k:k + 1, :] = jnp.concatenate([cre, cim], axis=1)


def _s5_pass_b(s_all, tab, n_lat_tiles):
    nc = s_all.shape[2]
    nt = nc // S5_TILE
    blk = (None, S5_NB, S5_TILE, S5_SB)
    oblk = (S5_NB, S5_TILE, S5_SB)
    return pl.pallas_call(
        _s5b_kernel,
        grid=(nt,),
        in_specs=[pl.BlockSpec(blk, lambda j: (0, 0, (j + n_lat_tiles) % nt, 0)),
                  pl.BlockSpec(blk, lambda j: (1, 0, nt - 1 - j, 0)),
                  pl.BlockSpec((2, S5_NB, 16, S5_SB), lambda j: (0, 0, 0, 0))],
        out_specs=[pl.BlockSpec(oblk, lambda j: (0, (j + n_lat_tiles) % nt, 0)),
                   pl.BlockSpec(oblk, lambda j: (0, nt - 1 - j, 0))],
        out_shape=[jax.ShapeDtypeStruct((S5_NB, nc, S5_SB), f32)] * 2,
        scratch_shapes=[pltpu.VMEM((2 * S5_NB, S5_SB), f32)],
        compiler_params=_cparams(("arbitrary",)),
        name="s5_b",
    )(s_all, s_all, tab)


def _s5c_kernel(*refs):
    u_refs = refs[:S5_Q]
    xf_ref, xr_ref, wt_ref, wc_ref, dsk_ref, y_ref = refs[S5_Q:]
    d = pl.program_id(1)
    b = pl.program_id(2)

    @pl.when((d == 0) & (b == 0))
    def _():
        for s, u in enumerate(u_refs):
            y_ref[:, s * S5_WIDTH:(s + 1) * S5_WIDTH] = u[...] * dsk_ref[...]

    for bb in range(S5_NB):
        @pl.when(b == bb)
        def _(bb=bb):
            ub = _u_block([u[...] for u in u_refs], bb).astype(bf16)
            xin = jnp.where(d == 0, xf_ref[bb], xr_ref[bb]).astype(bf16)
            yb = _dot(ub, wt_ref[...]) + _dot(xin, wc_ref[...])
            for s in range(S5_Q):
                lo = s * S5_WIDTH + bb * S5_LB
                y_ref[:, lo:lo + S5_LB] += yb[:, s * S5_LB:(s + 1) * S5_LB]


def _s5_pass_c(p8, xf, xr, w_t, w_c, dsk):
    nc = p8.shape[0]
    tk = _pick(nc, (208, 264, 176, 96, 48, 32))
    wspec = pl.BlockSpec((None, None, S5_SB, S5_SB), lambda c, d, b: (d, b, 0, 0))
    xspec = pl.BlockSpec((S5_NB, tk, S5_SB), lambda c, d, b: (0, c, 0))
    return pl.pallas_call(
        _s5c_kernel,
        grid=(nc // tk, 2, S5_NB),
        in_specs=_u_specs(tk) + [xspec, xspec, wspec, wspec,
                                 pl.BlockSpec((1, S5_WIDTH), lambda c, d, b: (0, 0))],
        out_specs=pl.BlockSpec((tk, S5_Q * S5_WIDTH), lambda c, d, b: (c, 0)),
        out_shape=jax.ShapeDtypeStruct((nc, S5_Q * S5_WIDTH), f32),
        compiler_params=_cparams(("arbitrary", "arbitrary", "arbitrary")),
        name="s5_c",
    )(*([p8] * S5_Q), xf, xr, w_t, w_c, dsk)


def s5_mix(p_all, tables, dsk, n_lat):
    w_a, w_t, w_c, tab = tables
    t = p_all.shape[0]
    p8 = p_all.reshape(t // S5_Q, S5_Q * PW)
    s_all = _s5_pass_a(p8, w_a)
    xf, xr = _s5_pass_b(s_all, tab, n_lat // (S5_Q * S5_TILE))
    y8 = _s5_pass_c(p8, xf, xr, w_t, w_c, dsk.reshape(1, S5_WIDTH))
    return y8.reshape(t, S5_WIDTH)


def _ssd_scan(x, dt, a, b, c, s0):
    bt, seq, g, r, p = x.shape
    n = b.shape[-1]
    nc = seq // SSD_CHUNK
    xc = x.reshape(bt, nc, SSD_CHUNK, g, r, p)
    dtc = dt.reshape(bt, nc, SSD_CHUNK, g, r)
    bc = b.reshape(bt, nc, SSD_CHUNK, g, n)
    acum = jnp.cumsum(dtc * a, axis=2)
    a_last = acum[:, :, -1]
    w_state = jnp.exp(a_last[:, :, None] - acum) * dtc
    chunk_states = jnp.einsum('bcsgn,bcsgr,bcsgrp->bcgrpn', bc, w_state, xc)

    def step(s, inp):
        st, decay = inp
        return decay[..., None, None] * s + st, s

    s_final, s_in = lax.scan(step, s0, (jnp.moveaxis(chunk_states, 1, 0), jnp.moveaxis(jnp.exp(a_last), 1, 0)))
    if c is None:
        return None, s_final
    s_in = jnp.moveaxis(s_in, 0, 1)
    cc = c.reshape(bt, nc, SSD_CHUNK, g, n)
    seg = acum[:, :, :, None] - acum[:, :, None]
    lower = jnp.tril(jnp.ones((SSD_CHUNK, SSD_CHUNK), dtype=bool))[:, :, None, None]
    decay = jnp.exp(jnp.where(lower, seg, -jnp.inf))
    cb = jnp.einsum('bcqgn,bcsgn->bcqsg', cc, bc)
    y_diag = jnp.einsum('bcqsgr,bcsgrp->bcqgrp', cb[..., None] * decay * dtc[:, :, None], xc)
    y_off = jnp.einsum('bcqgn,bcgrpn->bcqgrp', cc, s_in) * jnp.exp(acum)[..., None]
    return (y_diag + y_off).reshape(bt, seq, g, r, p), s_final


def _split_xbc(xbc):
    bt, l, ch = xbc.shape
    xs = xbc[..., :SSD_WIDTH].reshape(bt, l, SSD_GROUPS, SSD_HPG, SSD_HEADDIM)
    bs = xbc[..., SSD_WIDTH:SSD_WIDTH + SSD_BC].reshape(bt, l, SSD_GROUPS, SSD_STATE)
    cs = xbc[..., SSD_WIDTH + SSD_BC:].reshape(bt, l, SSD_GROUPS, SSD_STATE) if ch == CONV_CH else None
    return xs, bs, cs


def _ssd_branch(p_lat, p_ctx, rows, conv_w, conv_b, a_log, dt_bias, d, norm_w, ctx_out):
    bt, seq, _ = p_lat.shape
    n_ctx = p_ctx.shape[1]
    xb_w = OFF_DT - OFF_X
    s_lat = _to_colmajor(p_lat[..., OFF_X:OFF_Z], rows)
    xbc_lat = jnp.concatenate([s_lat[..., :xb_w], s_lat[..., OFF_C - OFF_X:]], axis=-1)
    dt_lat = s_lat[..., xb_w:OFF_C - OFF_X]
    xbc_lat = jax.nn.silu(_dwconv(xbc_lat, conv_w, conv_b))
    if ctx_out:
        xbc_ctx = jnp.concatenate([p_ctx[..., OFF_X:OFF_DT], p_ctx[..., OFF_C:OFF_Z]], axis=-1)
        xbc_ctx = jax.nn.silu(_dwconv(xbc_ctx, conv_w, conv_b))
    else:
        xbc_ctx = jax.nn.silu(_dwconv(p_ctx[..., OFF_X:OFF_DT], conv_w[:, :xb_w], conv_b[:xb_w]))
    dt_ctx = p_ctx[..., OFF_DT:OFF_C]
    x_l, b_l, c_l = _split_xbc(xbc_lat)
    x_c, b_c, c_c = _split_xbc(xbc_ctx)
    d_skip = d.reshape(SSD_GROUPS, SSD_HPG, 1)
    y_lat = x_l * d_skip
    y_ctx = x_c * d_skip if ctx_out else None
    zero = jnp.zeros((bt, SSD_GROUPS, SSD_HPG, SSD_HEADDIM, SSD_STATE), f32)
    for k, rev in enumerate((False, True)):
        a = -jnp.exp(a_log[k]).reshape(SSD_GROUPS, SSD_HPG)
        dtb = dt_bias[k]
        lo, hi = k * SSD_HEADS, (k + 1) * SSD_HEADS
        dt_l = jax.nn.softplus(dt_lat[..., lo:hi] + dtb).reshape(bt, seq, SSD_GROUPS, SSD_HPG)
        dt_c = jax.nn.softplus(dt_ctx[..., lo:hi] + dtb).reshape(bt, n_ctx, SSD_GROUPS, SSD_HPG)
        yc, s_ctx = _ssd_scan(_flip(x_c, rev), _flip(dt_c, rev), a, _flip(b_c, rev), _flip(c_c, rev), zero)
        yl, _ = _ssd_scan(_flip(x_l, rev), _flip(dt_l, rev), a, _flip(b_l, rev), _flip(c_l, rev), s_ctx)
        y_lat = y_lat + _flip(yl, rev)
        if ctx_out:
            y_ctx = y_ctx + _flip(yc, rev)
    y_lat = _from_colmajor(y_lat.reshape(bt, seq, SSD_WIDTH), rows)
    out_lat = _rms_norm(y_lat * jax.nn.silu(p_lat[..., OFF_Z:OFF_GATE_S5]), norm_w)
    if not ctx_out:
        return out_lat, None
    y_ctx = y_ctx.reshape(bt, n_ctx, SSD_WIDTH)
    out_ctx = _rms_norm(y_ctx * jax.nn.silu(p_ctx[..., OFF_Z:OFF_GATE_S5]), norm_w)
    return out_lat, out_ctx


def _merge(p, ya, yb, w_branch_s5, w_branch_ssd, w_out):
    bt, l, _ = p.shape
    g_a = jax.nn.sigmoid(p[..., OFF_GATE_S5:OFF_GATE_SSD]).reshape(bt * l, D_MODEL)
    g_b = jax.nn.sigmoid(p[..., OFF_GATE_SSD:PROJ_WIDTH]).reshape(bt * l, D_MODEL)
    za = matmul(ya.reshape(bt * l, S5_WIDTH), w_branch_s5)
    zb = matmul(yb.reshape(bt * l, SSD_WIDTH), w_branch_ssd)
    return matmul(g_a * za + g_b * zb, w_out).reshape(bt, l, D_MODEL)


def _expert_choice(h, w_router, w_gate, w_up, w_down):
    n_tok = h.shape[0]
    cap = CAPACITY_FACTOR * n_tok // N_EXPERTS
    aff = jax.nn.softmax((h @ w_router).astype(f32), axis=-1)
    g, idx = lax.top_k(aff.T, cap)
    xe = h[idx]
    hid = swiglu_bmm(xe, w_gate, w_up)
    ye = bmm(hid, w_down) * g[..., None]
    return jnp.zeros_like(h).at[idx.reshape(-1)].add(ye.reshape(-1, h.shape[-1]))


def kernel(x, c, ctx, c_ctx, w_ada, b_ada, w_in, s5_lambda_re, s5_lambda_im, s5_log_step, s5_b_re, s5_b_im,
           s5_c_re, s5_c_im, s5_d, s5_w_glu, ssd_conv_w, ssd_conv_b, ssd_a_log, ssd_dt_bias, ssd_d, ssd_norm_w,
           w_branch_s5, w_branch_ssd, w_out, ln1_g, ln1_b, w_router, w_exp_gate, w_exp_up, w_exp_down,
           ln2_g, ln2_b):
    bt, seq, _ = x.shape
    n_ctx = ctx.shape[1]
    rows = seq // GRID_W
    c8 = jnp.zeros((8, D_MODEL), f32).at[0].set(c[0]).at[1].set(c_ctx)
    mods = ada_all(c8, w_ada, b_ada)
    w_pad = jnp.pad(w_in, ((0, 0), (0, 0), (0, PW - PROJ_WIDTH))).astype(bf16)
    ctx_h = ctx
    for i in range(DEPTH):
        ctx_out = i < DEPTH - 1
        sh1, sc1, g1, sh2, sc2, g2 = jnp.split(mods[i, 0], N_MOD)
        csh1, csc1, cg1, csh2, csc2, cg2 = jnp.split(mods[i, 1], N_MOD)
        modrows = jnp.zeros((8, D_MODEL), f32).at[0].set(sh1).at[1].set(sc1).at[2].set(csh1).at[3].set(csc1)
        x_all = jnp.concatenate([x[0], ctx_h[0]], axis=0)
        p_all = proj(x_all, modrows, w_pad[i], seq)
        p_lat = p_all[None, :seq, :PROJ_WIDTH]
        p_ctx = p_all[None, seq:, :PROJ_WIDTH]
        s5_tab = _s5_tables(s5_lambda_re[i], s5_lambda_im[i], s5_log_step[i], s5_b_re[i], s5_b_im[i],
                            s5_c_re[i], s5_c_im[i])
        y_s5 = s5_mix(p_all, s5_tab, s5_d[i], seq)
        a_lat = _s5_glu(y_s5[None, :seq], s5_w_glu[i])
        a_ctx = _s5_glu(y_s5[None, seq:], s5_w_glu[i]) if ctx_out else None
        b_lat, b_ctx = _ssd_branch(p_lat, p_ctx, rows, ssd_conv_w[i], ssd_conv_b[i], ssd_a_log[i], ssd_dt_bias[i],
                                   ssd_d[i], ssd_norm_w[i], ctx_out)
        m_lat = _merge(p_lat, a_lat, b_lat, w_branch_s5[i], w_branch_ssd[i], w_out[i])
        x = _layer_norm_affine(ALPHA * x + g1 * m_lat, ln1_g[i], ln1_b[i])
        y_moe = _expert_choice(_modulate(x, sh2, sc2)[0], w_router[i], w_exp_gate[i], w_exp_up[i], w_exp_down[i])
        x = _layer_norm_affine(ALPHA * x + g2 * y_moe[None], ln2_g[i], ln2_b[i])
        if ctx_out:
            m_ctx = _merge(p_ctx, a_ctx, b_ctx, w_branch_s5[i], w_branch_ssd[i], w_out[i])
            ctx_h = _layer_norm_affine(ALPHA * ctx_h + cg1 * m_ctx, ln1_g[i], ln1_b[i])
            c_moe = _expert_choice(_modulate(ctx_h, csh2, csc2)[0], w_router[i], w_exp_gate[i], w_exp_up[i],
                                   w_exp_down[i])
            ctx_h = _layer_norm_affine(ALPHA * ctx_h + cg2 * c_moe[None], ln2_g[i], ln2_b[i])
    return x
```
